```python
import jax
import jax.numpy as jnp
from jax import lax
import numpy as np

D_MODEL = 1024
BATCH = 2
SEQ = 8192
DEPTH = 1

CHUNK = 64
LEFT_CHUNKS = 8
N_HEADS = 8
HEAD_DIM = 64
ATTN_WIDTH = N_HEADS * HEAD_DIM
POOL_WINDOWS = (2, 4, 8, 16)
POOL_GROUPS = len(POOL_WINDOWS)
POOL_WIDTH = 512
POOL_GROUP_DIM = POOL_WIDTH // POOL_GROUPS
REL_CLIP = 128
D_FF = 2816
N_BRANCHES = 2
IN_COLS = 3 * ATTN_WIDTH + POOL_WIDTH + N_BRANCHES * D_MODEL
N_ADA = 9
EPS = 1e-6

kernel_name = 'hybrid_chunk_attn_pool_block'


def rms_norm(x, g):
    xf = x.astype(jnp.float32)
    y = xf * lax.rsqrt(jnp.mean(xf * xf, axis=-1, keepdims=True) + EPS)
    return (y * g.astype(jnp.float32)).astype(x.dtype)


def modulate(h, shift, scale):
    return h * (1 + scale) + shift


def swiglu(h, w_in, w_out):
    a, b = jnp.split(h @ w_in, 2, axis=-1)
    return (jax.nn.silu(a) * b) @ w_out


def chunk_band_attention(q, k, v, q_gain, k_gain, rel_bias):
    b, s, _ = q.shape
    nc = s // CHUNK
    band = (LEFT_CHUNKS + 1) * CHUNK
    q = rms_norm(q.reshape(b, nc, CHUNK, N_HEADS, HEAD_DIM), q_gain)
    k = rms_norm(k.reshape(b, nc, CHUNK, N_HEADS, HEAD_DIM), k_gain)
    v = v.reshape(b, nc, CHUNK, N_HEADS, HEAD_DIM)
    pad = ((0, 0), (LEFT_CHUNKS, 0), (0, 0), (0, 0), (0, 0))
    kp = jnp.pad(k, pad)
    vp = jnp.pad(v, pad)
    kb = jnp.concatenate([kp[:, w:w + nc] for w in range(LEFT_CHUNKS + 1)], axis=2)
    vb = jnp.concatenate([vp[:, w:w + nc] for w in range(LEFT_CHUNKS + 1)], axis=2)
    scores = jnp.einsum('bnqhd,bnkhd->bnhqk', q, kb).astype(jnp.float32) * (HEAD_DIM ** -0.5)
    r = np.arange(CHUNK)
    j = np.arange(band)
    dist = (LEFT_CHUNKS - j // CHUNK)[None, :] * CHUNK + r[:, None] - (j % CHUNK)[None, :]
    idx = np.clip(dist, -REL_CLIP, REL_CLIP) + REL_CLIP
    bias = rel_bias[:, idx].astype(jnp.float32)
    key_chunk = np.arange(nc)[:, None] - LEFT_CHUNKS + (j // CHUNK)[None, :]
    valid = jnp.asarray(key_chunk >= 0)[None, :, None, None, :]
    scores = jnp.where(valid, scores + bias[None, None], -1e30)
    p = jax.nn.softmax(scores, axis=-1).astype(v.dtype)
    out = jnp.einsum('bnhqk,bnkhd->bnqhd', p, vb)
    return out.reshape(b, s, ATTN_WIDTH)


def multiscale_pool(u, w_group, scale):
    b, s, _ = u.shape
    ug = u.reshape(b, s, POOL_GROUPS, POOL_GROUP_DIM).astype(jnp.float32)
    cs = jnp.pad(jnp.cumsum(ug, axis=1), ((0, 0), (1, 0), (0, 0), (0, 0)))
    t = np.arange(s)[:, None]
    win = np.array(POOL_WINDOWS)[None, :]
    start = np.maximum(t + 1 - win, 0)
    count = (t + 1 - start).astype(np.float32)
    lo = cs[:, start, np.arange(POOL_GROUPS)[None, :], :]
    mean = (cs[:, 1:] - lo) / count[None, :, :, None]
    mixed = (mean - ug).astype(u.dtype)
    y = jnp.einsum('bsgc,gcd->bsgd', mixed, w_group).reshape(b, s, POOL_WIDTH)
    return y * scale


def token_mix(h, w_in, q_gain, k_gain, rel_bias, w_attn_out, w_pool_group, pool_scale, w_pool_out, w_o):
    z = h @ w_in
    a3 = 3 * ATTN_WIDTH
    q, k, v, u, ga, gb = jnp.split(
        z, [ATTN_WIDTH, 2 * ATTN_WIDTH, a3, a3 + POOL_WIDTH, a3 + POOL_WIDTH + D_MODEL], axis=-1)
    ya = chunk_band_attention(q, k, v, q_gain, k_gain, rel_bias) @ w_attn_out
    yb = multiscale_pool(u, w_pool_group, pool_scale) @ w_pool_out
    merged = jax.nn.sigmoid(ga) * ya + jax.nn.sigmoid(gb) * yb
    return merged @ w_o


def setup_inputs(seed: int = 0) -> dict:
    key = jax.random.key(seed)
    ks = jax.random.split(key, 24)
    f32 = jnp.float32
    L = DEPTH

    def nrm(k, shape, scale):
        return jax.random.normal(k, shape, f32) * scale

    def gain(k, shape):
        return 1.0 + 0.05 * jax.random.normal(k, shape, f32)

    return {
        'x': nrm(ks[0], (BATCH, SEQ, D_MODEL), 1.0),
        'c': nrm(ks[1], (BATCH, D_MODEL), 1.0),
        'w_ada': nrm(ks[2], (L, D_MODEL, N_ADA * D_MODEL), D_MODEL ** -0.5),
        'b_ada': nrm(ks[3], (L, N_ADA * D_MODEL), 0.02),
        'g_ffn1': gain(ks[4], (L, D_MODEL)),
        'w_ffn1_in': nrm(ks[5], (L, D_MODEL, 2 * D_FF), D_MODEL ** -0.5),
        'w_ffn1_out': nrm(ks[6], (L, D_FF, D_MODEL), D_FF ** -0.5),
        'g_mix': gain(ks[7], (L, D_MODEL)),
        'w_in': nrm(ks[8], (L, D_MODEL, IN_COLS), D_MODEL ** -0.5),
        'q_gain': gain(ks[9], (L, HEAD_DIM)),
        'k_gain': gain(ks[10], (L, HEAD_DIM)),
        'rel_bias': nrm(ks[11], (L, N_HEADS, 2 * REL_CLIP + 1), 0.5),
        'w_attn_out': nrm(ks[12], (L, ATTN_WIDTH, D_MODEL), ATTN_WIDTH ** -0.5),
        'w_pool_group': nrm(ks[13], (L, POOL_GROUPS, POOL_GROUP_DIM, POOL_GROUP_DIM), POOL_GROUP_DIM ** -0.5),
        'pool_scale': gain(ks[14], (L, POOL_WIDTH)),
        'w_pool_out': nrm(ks[15], (L, POOL_WIDTH, D_MODEL), POOL_WIDTH ** -0.5),
        'w_o': nrm(ks[16], (L, D_MODEL, D_MODEL), D_MODEL ** -0.5),
        'g_ffn2': gain(ks[17], (L, D_MODEL)),
        'w_ffn2_in': nrm(ks[18], (L, D_MODEL, 2 * D_FF), D_MODEL ** -0.5),
        'w_ffn2_out': nrm(ks[19], (L, D_FF, D_MODEL), D_FF ** -0.5),
    }


def reference(x, c, w_ada, b_ada, g_ffn1, w_ffn1_in, w_ffn1_out, g_mix, w_in, q_gain, k_gain,
              rel_bias, w_attn_out, w_pool_group, pool_scale, w_pool_out, w_o, g_ffn2,
              w_ffn2_in, w_ffn2_out):
    b = x.shape[0]
    cc = jax.nn.silu(c)
    for l in range(DEPTH):
        mod = (cc @ w_ada[l] + b_ada[l]).reshape(b, N_ADA, D_MODEL)
        sh1, sc1, gt1, sh2, sc2, gt2, sh3, sc3, gt3 = [mod[:, i, None, :] for i in range(N_ADA)]
        h = modulate(rms_norm(x, g_ffn1[l]), sh1, sc1)
        x = x + 0.5 * gt1 * swiglu(h, w_ffn1_in[l], w_ffn1_out[l])
        h = modulate(rms_norm(x, g_mix[l]), sh2, sc2)
        x = x + gt2 * token_mix(h, w_in[l], q_gain[l], k_gain[l], rel_bias[l], w_attn_out[l],
                                w_pool_group[l], pool_scale[l], w_pool_out[l], w_o[l])
        h = modulate(rms_norm(x, g_ffn2[l]), sh3, sc3)
        x = x + 0.5 * gt3 * swiglu(h, w_ffn2_in[l], w_ffn2_out[l])
    return x
```

```python
import functools

import jax
import jax.numpy as jnp
from jax import lax
from jax.experimental import pallas as pl
from jax.experimental.pallas import tpu as pltpu

D_MODEL = 1024
CHUNK = 64
LEFT_CHUNKS = 8
N_HEADS = 8
HEAD_DIM = 64
ATTN_WIDTH = N_HEADS * HEAD_DIM
POOL_WINDOWS = (2, 4, 8, 16)
POOL_GROUPS = len(POOL_WINDOWS)
POOL_WIDTH = 512
POOL_GROUP_DIM = POOL_WIDTH // POOL_GROUPS
REL_CLIP = 128
D_FF = 2816
N_ADA = 9
EPS = 1e-6
MASK_VALUE = -1e30

LANES = 128
MXU_DIM = 256
VMEM_LIMIT_BYTES = 56 * 1024 * 1024

ROW_TILE = 512
FF_CHUNK = MXU_DIM
N_FF_CHUNKS = D_FF // FF_CHUNK
Q_BLOCK = 2 * CHUNK
KEY_WINDOW = Q_BLOCK + LEFT_CHUNKS * CHUNK
N_Q_BLOCKS = ROW_TILE // Q_BLOCK
POOL_HISTORY = 16
MOD_COLS = 1152

assert D_FF % FF_CHUNK == 0
assert ROW_TILE == LEFT_CHUNKS * CHUNK
assert POOL_HISTORY >= max(POOL_WINDOWS) - 1 and POOL_HISTORY % 8 == 0
assert (N_ADA * D_MODEL) % MOD_COLS == 0 and MOD_COLS % LANES == 0


def _resident(shape):
    zeros = (0,) * len(shape)
    return pl.BlockSpec(shape, lambda *_: zeros, pipeline_mode=pl.Buffered(1))


def _mod_kernel(ct_ref, w_ref, b_ref, o_ref):
    n_batch = ct_ref.shape[1]
    k_rows = 256
    for b in range(n_batch):
        acc = b_ref[...]
        for k0 in range(0, D_MODEL, k_rows):
            col = ct_ref[k0:k0 + k_rows, b:b + 1]
            col = col * jax.nn.sigmoid(col)
            acc = acc + jnp.sum(w_ref[k0:k0 + k_rows, :] * col, axis=0, keepdims=True)
        o_ref[b:b + 1, :] = acc


def _modulation(c, w_ada, b_ada):
    n_batch = c.shape[0]
    n_cols = w_ada.shape[1]
    return pl.pallas_call(
        _mod_kernel,
        grid=(n_cols // MOD_COLS,),
        in_specs=[
            pl.BlockSpec((D_MODEL, n_batch), lambda j: (0, 0)),
            pl.BlockSpec((D_MODEL, MOD_COLS), lambda j: (0, j)),
            pl.BlockSpec((1, MOD_COLS), lambda j: (0, j)),
        ],
        out_specs=pl.BlockSpec((n_batch, MOD_COLS), lambda j: (0, j)),
        out_shape=jax.ShapeDtypeStruct((n_batch, n_cols), jnp.float32),
        compiler_params=pltpu.CompilerParams(
            dimension_semantics=("arbitrary",), vmem_limit_bytes=VMEM_LIMIT_BYTES),
        name="adaln_modulation",
    )(c.T, w_ada, b_ada.reshape(1, n_cols))


def _norm_modulate(x, g, shift, scale):
    ms = jnp.mean(x * x, axis=-1, keepdims=True)
    y = x * lax.rsqrt(ms + EPS) * g
    return (y * (1.0 + scale) + shift).astype(jnp.bfloat16)


def _ffn_kernel(x_ref, mod_ref, g_ref, win_ref, wout_ref, o_ref, *, mod_row):
    x = x_ref[...]
    shift = mod_ref[0, mod_row:mod_row + 1, :]
    scale = mod_ref[0, mod_row + 1:mod_row + 2, :]
    gate = mod_ref[0, mod_row + 2:mod_row + 3, :]
    h = _norm_modulate(x, g_ref[...], shift, scale)
    acc = None
    for c in range(N_FF_CHUNKS):
        ab = jnp.dot(h, win_ref[c], preferred_element_type=jnp.float32)
        a = ab[:, :FF_CHUNK]
        b = ab[:, FF_CHUNK:]
        act = (a * jax.nn.sigmoid(a) * b).astype(jnp.bfloat16)
        part = jnp.dot(act, wout_ref[c], preferred_element_type=jnp.float32)
        acc = part if acc is None else acc + part
    o_ref[...] = x + (0.5 * gate) * acc


def _ffn(x2d, mod, g, w_in, w_out, *, mod_row, tiles_per_batch):
    n_rows = x2d.shape[0]
    w_a = w_in[:, :D_FF].reshape(D_MODEL, N_FF_CHUNKS, FF_CHUNK)
    w_b = w_in[:, D_FF:].reshape(D_MODEL, N_FF_CHUNKS, FF_CHUNK)
    w_in_c = jnp.concatenate([w_a, w_b], axis=-1).transpose(1, 0, 2).astype(jnp.bfloat16)
    w_out_c = w_out.reshape(N_FF_CHUNKS, FF_CHUNK, D_MODEL).astype(jnp.bfloat16)
    return pl.pallas_call(
        functools.partial(_ffn_kernel, mod_row=mod_row),
        grid=(n_rows // ROW_TILE,),
        in_specs=[
            pl.BlockSpec((ROW_TILE, D_MODEL), lambda i: (i, 0)),
            pl.BlockSpec((1, N_ADA, D_MODEL), lambda i: (i // tiles_per_batch, 0, 0)),
            _resident((1, D_MODEL)),
            _resident((N_FF_CHUNKS, D_MODEL, 2 * FF_CHUNK)),
            _resident((N_FF_CHUNKS, FF_CHUNK, D_MODEL)),
        ],
        out_specs=pl.BlockSpec((ROW_TILE, D_MODEL), lambda i: (i, 0)),
        out_shape=jax.ShapeDtypeStruct(x2d.shape, x2d.dtype),
        compiler_params=pltpu.CompilerParams(
            dimension_semantics=("arbitrary",), vmem_limit_bytes=VMEM_LIMIT_BYTES),
        name="swiglu_sublayer",
    )(x2d, mod, g.reshape(1, D_MODEL), w_in_c, w_out_c)


def _head_rms_norm(t, gain):
    lane = lax.broadcasted_iota(jnp.int32, (t.shape[0], LANES), 1)
    low = lane < HEAD_DIM
    parts = []
    for p in range(ATTN_WIDTH // LANES):
        blk = t[:, p * LANES:(p + 1) * LANES]
        sq = blk * blk
        ms_low = jnp.sum(jnp.where(low, sq, 0.0), axis=-1, keepdims=True)
        ms_high = jnp.sum(jnp.where(low, 0.0, sq), axis=-1, keepdims=True)
        ms = jnp.where(low, ms_low, ms_high) * (1.0 / HEAD_DIM)
        parts.append(blk * lax.rsqrt(ms + EPS))
    return jnp.concatenate(parts, axis=-1) * gain


def _mix_kernel(x_ref, mod_ref, g_ref, win_ref, qg_ref, kg_ref, bias_ref, wa_ref, wg_ref,
                ps_ref, wp_ref, wo_ref, o_ref, k_buf, v_buf, u_buf, a_buf):
    tile = pl.program_id(1)
    first = tile == 0

    @pl.when(first)
    def _():
        k_buf[0:ROW_TILE, :] = jnp.zeros((ROW_TILE, ATTN_WIDTH), k_buf.dtype)
        v_buf[0:ROW_TILE, :] = jnp.zeros((ROW_TILE, ATTN_WIDTH), v_buf.dtype)
        u_buf[0:POOL_HISTORY, :] = jnp.zeros((POOL_HISTORY, POOL_WIDTH), u_buf.dtype)

    @pl.when(jnp.logical_not(first))
    def _():
        k_buf[0:ROW_TILE, :] = k_buf[ROW_TILE:2 * ROW_TILE, :]
        v_buf[0:ROW_TILE, :] = v_buf[ROW_TILE:2 * ROW_TILE, :]
        u_buf[0:POOL_HISTORY, :] = u_buf[ROW_TILE:ROW_TILE + POOL_HISTORY, :]

    x = x_ref[...]
    shift = mod_ref[0, 3:4, :]
    scale = mod_ref[0, 4:5, :]
    gate = mod_ref[0, 5:6, :]
    h = _norm_modulate(x, g_ref[...], shift, scale)

    def proj(col0, width):
        return jnp.dot(h, win_ref[:, col0:col0 + width], preferred_element_type=jnp.float32)

    q = proj(0, ATTN_WIDTH)
    k = proj(ATTN_WIDTH, ATTN_WIDTH)
    v = proj(2 * ATTN_WIDTH, ATTN_WIDTH)
    u = proj(3 * ATTN_WIDTH, POOL_WIDTH)

    qn = (_head_rms_norm(q, qg_ref[...]) * (HEAD_DIM ** -0.5)).astype(jnp.bfloat16)
    k_buf[ROW_TILE:2 * ROW_TILE, :] = _head_rms_norm(k, kg_ref[...]).astype(jnp.bfloat16)
    v_buf[ROW_TILE:2 * ROW_TILE, :] = v.astype(jnp.bfloat16)

    lane = lax.broadcasted_iota(jnp.int32, (Q_BLOCK, LANES), 1)
    low = lane < HEAD_DIM
    key_col = lax.broadcasted_iota(jnp.int32, (1, KEY_WINDOW), 1)
    for qb in range(N_Q_BLOCKS):
        row0 = qb * Q_BLOCK
        missing = jnp.logical_and(first, key_col < ROW_TILE - row0)
        col_mask = jnp.where(missing, MASK_VALUE, 0.0)
        for p in range(ATTN_WIDTH // LANES):
            cols = slice(p * LANES, (p + 1) * LANES)
            q_pair = qn[row0:row0 + Q_BLOCK, cols]
            k_win = k_buf[row0:row0 + KEY_WINDOW, cols]
            v_win = v_buf[row0:row0 + KEY_WINDOW, cols]
            outs = []
            for e in range(2):
                keep = low if e == 0 else jnp.logical_not(low)
                q_head = jnp.where(keep, q_pair, jnp.zeros_like(q_pair))
                s = lax.dot_general(q_head, k_win, (((1,), (1,)), ((), ())),
                                    preferred_element_type=jnp.float32)
                s = s + bias_ref[2 * p + e] + col_mask
                m = jnp.max(s, axis=-1, keepdims=True)
                pexp = jnp.exp(s - m)
                denom = jnp.sum(pexp, axis=-1, keepdims=True)
                o = jnp.dot(pexp.astype(jnp.bfloat16), v_win, preferred_element_type=jnp.float32)
                outs.append(o / denom)
            a_buf[row0:row0 + Q_BLOCK, cols] = jnp.where(low, outs[0], outs[1]).astype(a_buf.dtype)
    ya = jnp.dot(a_buf[...], wa_ref[...], preferred_element_type=jnp.float32)

    u_buf[POOL_HISTORY:POOL_HISTORY + ROW_TILE, :] = u
    pos = tile * ROW_TILE + lax.broadcasted_iota(jnp.int32, (ROW_TILE, 1), 0)
    pooled = []
    for gi, window in enumerate(POOL_WINDOWS):
        cols = slice(gi * POOL_GROUP_DIM, (gi + 1) * POOL_GROUP_DIM)
        total = None
        for j in range(window):
            term = u_buf[POOL_HISTORY - j:POOL_HISTORY - j + ROW_TILE, cols]
            total = term if total is None else total + term
        count = jnp.minimum(pos + 1, window).astype(jnp.float32)
        mixed = (total / count - u[:, cols]).astype(jnp.bfloat16)
        y = jnp.dot(mixed, wg_ref[gi], preferred_element_type=jnp.float32)
        pooled.append((y * ps_ref[:, cols]).astype(jnp.bfloat16))
    yb = jnp.dot(jnp.concatenate(pooled, axis=-1), wp_ref[...], preferred_element_type=jnp.float32)

    ga = proj(3 * ATTN_WIDTH + POOL_WIDTH, D_MODEL)
    gb = proj(3 * ATTN_WIDTH + POOL_WIDTH + D_MODEL, D_MODEL)
    merged = (jax.nn.sigmoid(ga) * ya + jax.nn.sigmoid(gb) * yb).astype(jnp.bfloat16)
    o_ref[...] = x + gate * jnp.dot(merged, wo_ref[...], preferred_element_type=jnp.float32)


def _band_bias(rel_bias):
    r = jnp.arange(Q_BLOCK)[:, None]
    j = jnp.arange(KEY_WINDOW)[None, :]
    dist = LEFT_CHUNKS * CHUNK + r - j
    idx = jnp.clip(dist, -REL_CLIP, REL_CLIP) + REL_CLIP
    rel_chunk = r // CHUNK + LEFT_CHUNKS - j // CHUNK
    in_band = jnp.logical_and(rel_chunk >= 0, rel_chunk <= LEFT_CHUNKS)
    return jnp.where(in_band[None], rel_bias[:, idx].astype(jnp.float32), MASK_VALUE)


def _token_mix(x3d, mod, g, w_in, q_gain, k_gain, rel_bias, w_attn_out, w_pool_group, pool_scale,
               w_pool_out, w_o):
    n_batch, seq, _ = x3d.shape
    bf16 = jnp.bfloat16
    in_cols = w_in.shape[1]
    return pl.pallas_call(
        _mix_kernel,
        grid=(n_batch, seq // ROW_TILE),
        in_specs=[
            pl.BlockSpec((None, ROW_TILE, D_MODEL), lambda b, i: (b, i, 0)),
            pl.BlockSpec((1, N_ADA, D_MODEL), lambda b, i: (b, 0, 0)),
            _resident((1, D_MODEL)),
            _resident((D_MODEL, in_cols)),
            _resident((1, ATTN_WIDTH)),
            _resident((1, ATTN_WIDTH)),
            _resident((N_HEADS, Q_BLOCK, KEY_WINDOW)),
            _resident((ATTN_WIDTH, D_MODEL)),
            _resident((POOL_GROUPS, POOL_GROUP_DIM, POOL_GROUP_DIM)),
            _resident((1, POOL_WIDTH)),
            _resident((POOL_WIDTH, D_MODEL)),
            _resident((D_MODEL, D_MODEL)),
        ],
        out_specs=pl.BlockSpec((None, ROW_TILE, D_MODEL), lambda b, i: (b, i, 0)),
        out_shape=jax.ShapeDtypeStruct(x3d.shape, x3d.dtype),
        scratch_shapes=[
            pltpu.VMEM((2 * ROW_TILE, ATTN_WIDTH), bf16),
            pltpu.VMEM((2 * ROW_TILE, ATTN_WIDTH), bf16),
            pltpu.VMEM((POOL_HISTORY + ROW_TILE, POOL_WIDTH), jnp.float32),
            pltpu.VMEM((ROW_TILE, ATTN_WIDTH), bf16),
        ],
        compiler_params=pltpu.CompilerParams(
            dimension_semantics=("arbitrary", "arbitrary"), vmem_limit_bytes=VMEM_LIMIT_BYTES),
        name="token_mix_sublayer",
    )(x3d, mod, g.reshape(1, D_MODEL), w_in.astype(bf16),
      jnp.tile(q_gain, N_HEADS).reshape(1, ATTN_WIDTH), jnp.tile(k_gain, N_HEADS).reshape(1, ATTN_WIDTH),
      _band_bias(rel_bias), w_attn_out.astype(bf16), w_pool_group.astype(bf16),
      pool_scale.reshape(1, POOL_WIDTH), w_pool_out.astype(bf16), w_o.astype(bf16))


def kernel(x, c, w_ada, b_ada, g_ffn1, w_ffn1_in, w_ffn1_out, g_mix, w_in, q_gain, k_gain, rel_bias,
           w_attn_out, w_pool_group, pool_scale, w_pool_out, w_o, g_ffn2, w_ffn2_in, w_ffn2_out):
    n_batch, seq, d_model = x.shape
    depth = w_ada.shape[0]
    assert d_model == D_MODEL and seq % ROW_TILE == 0
    tiles_per_batch = seq // ROW_TILE
    for l in range(depth):
        mod = _modulation(c, w_ada[l], b_ada[l]).reshape(n_batch, N_ADA, D_MODEL)
        x2d = x.reshape(n_batch * seq, D_MODEL)
        x2d = _ffn(x2d, mod, g_ffn1[l], w_ffn1_in[l], w_ffn1_out[l], mod_row=0,
                   tiles_per_batch=tiles_per_batch)
        x = _token_mix(x2d.reshape(n_batch, seq, D_MODEL), mod, g_mix[l], w_in[l], q_gain[l], k_gain[l],
                       rel_bias[l], w_attn_out[l], w_pool_group[l], pool_scale[l], w_pool_out[l], w_o[l])
        x2d = _ffn(x.reshape(n_batch * seq, D_MODEL), mod, g_ffn2[l], w_ffn2_in[l], w_ffn2_out[l],
                   mod_row=6, tiles_per_batch=tiles_per_batch)
        x = x2d.reshape(n_batch, seq, D_MODEL)
    return x
```

```python
import functools
import math

import jax
import jax.numpy as jnp
from jax import lax
from jax.experimental import pallas as pl
from jax.experimental.pallas import tpu as pltpu

D_MODEL = 1024
CHUNK = 64
LEFT_CHUNKS = 8
N_HEADS = 8
HEAD_DIM = 64
ATTN_WIDTH = N_HEADS * HEAD_DIM
POOL_WINDOWS = (2, 4, 8, 16)
POOL_GROUPS = len(POOL_WINDOWS)
POOL_WIDTH = 512
POOL_GROUP_DIM = POOL_WIDTH // POOL_GROUPS
REL_CLIP = 128
D_FF = 2816
N_ADA = 9
EPS = 1e-6
MASK_VALUE = -1e30
LOG2_E = math.log2(math.e)

LANES = 128
MXU_DIM = 256
VMEM_LIMIT_BYTES = 56 * 1024 * 1024

ROW_TILE = 512
FF_CHUNK = MXU_DIM
N_FF_CHUNKS = D_FF // FF_CHUNK
BAND = LEFT_CHUNKS * CHUNK
Q_BLOCK = 2 * CHUNK
KEY_WINDOW = Q_BLOCK + BAND
QK_ROWS = 2 * Q_BLOCK
QK_SPAN = QK_ROWS + BAND
HEAD_PAIRS = ATTN_WIDTH // LANES
BIAS_PERIOD = 768
POOL_HISTORY = 16
MOD_COLS = 1152

assert D_FF % FF_CHUNK == 0
assert ROW_TILE == BAND
assert ROW_TILE % QK_ROWS == 0 and 2 * HEAD_DIM == LANES
assert BIAS_PERIOD >= Q_BLOCK + KEY_WINDOW - 1 and BIAS_PERIOD % LANES == 0
assert POOL_HISTORY >= max(POOL_WINDOWS) - 1 and POOL_HISTORY % 8 == 0
assert (N_ADA * D_MODEL) % MOD_COLS == 0 and MOD_COLS % LANES == 0


def _resident(shape):
    zeros = (0,) * len(shape)
    return pl.BlockSpec(shape, lambda *_: zeros, pipeline_mode=pl.Buffered(1))


def _mod_kernel(ct_ref, w_ref, b_ref, o_ref):
    n_batch = ct_ref.shape[1]
    k_rows = 256
    for b in range(n_batch):
        acc = b_ref[...]
        for k0 in range(0, D_MODEL, k_rows):
            col = ct_ref[k0:k0 + k_rows, b:b + 1]
            col = col * jax.nn.sigmoid(col)
            acc = acc + jnp.sum(w_ref[k0:k0 + k_rows, :] * col, axis=0, keepdims=True)
        o_ref[b:b + 1, :] = acc


def _modulation(c, w_ada, b_ada):
    n_batch = c.shape[0]
    n_cols = w_ada.shape[1]
    return pl.pallas_call(
        _mod_kernel,
        grid=(n_cols // MOD_COLS,),
        in_specs=[
            pl.BlockSpec((D_MODEL, n_batch), lambda j: (0, 0)),
            pl.BlockSpec((D_MODEL, MOD_COLS), lambda j: (0, j)),
            pl.BlockSpec((1, MOD_COLS), lambda j: (0, j)),
        ],
        out_specs=pl.BlockSpec((n_batch, MOD_COLS), lambda j: (0, j)),
        out_shape=jax.ShapeDtypeStruct((n_batch, n_cols), jnp.float32),
        compiler_params=pltpu.CompilerParams(
            dimension_semantics=("arbitrary",), vmem_limit_bytes=VMEM_LIMIT_BYTES),
        name="adaln_modulation",
    )(c.T, w_ada, b_ada.reshape(1, n_cols))


def _norm_modulate(x, g, shift, scale):
    ms = jnp.mean(x * x, axis=-1, keepdims=True)
    y = x * lax.rsqrt(ms + EPS) * g
    return (y * (1.0 + scale) + shift).astype(jnp.bfloat16)


def _ffn_kernel(x_ref, mod_ref, g_ref, win_ref, wout_ref, o_ref, *, mod_row):
    x = x_ref[...]
    shift = mod_ref[0, mod_row:mod_row + 1, :]
    scale = mod_ref[0, mod_row + 1:mod_row + 2, :]
    gate = mod_ref[0, mod_row + 2:mod_row + 3, :]
    h = _norm_modulate(x, g_ref[...], shift, scale)
    acc = None
    for c in range(N_FF_CHUNKS):
        lo = c * FF_CHUNK
        a = jnp.dot(h, win_ref[:, lo:lo + FF_CHUNK], preferred_element_type=jnp.float32)
        b = jnp.dot(h, win_ref[:, D_FF + lo:D_FF + lo + FF_CHUNK], preferred_element_type=jnp.float32)
        act = (a * jax.nn.sigmoid(a) * b).astype(jnp.bfloat16)
        part = jnp.dot(act, wout_ref[lo:lo + FF_CHUNK, :], preferred_element_type=jnp.float32)
        acc = part if acc is None else acc + part
    o_ref[...] = x + (0.5 * gate) * acc


def _ffn(x2d, mod, g, w_in, w_out, *, mod_row, tiles_per_batch):
    n_rows = x2d.shape[0]
    return pl.pallas_call(
        functools.partial(_ffn_kernel, mod_row=mod_row),
        grid=(n_rows // ROW_TILE,),
        in_specs=[
            pl.BlockSpec((ROW_TILE, D_MODEL), lambda i: (i, 0)),
            pl.BlockSpec((1, N_ADA, D_MODEL), lambda i: (i // tiles_per_batch, 0, 0)),
            _resident((1, D_MODEL)),
            _resident((D_MODEL, 2 * D_FF)),
            _resident((D_FF, D_MODEL)),
        ],
        out_specs=pl.BlockSpec((ROW_TILE, D_MODEL), lambda i: (i, 0)),
        out_shape=jax.ShapeDtypeStruct(x2d.shape, x2d.dtype),
        compiler_params=pltpu.CompilerParams(
            dimension_semantics=("arbitrary",), vmem_limit_bytes=VMEM_LIMIT_BYTES),
        name="swiglu_sublayer",
    )(x2d, mod, g.reshape(1, D_MODEL), w_in.astype(jnp.bfloat16), w_out.astype(jnp.bfloat16))


def _head_rms_norm(t, gain):
    lane = lax.broadcasted_iota(jnp.int32, (t.shape[0], LANES), 1)
    low = lane < HEAD_DIM
    parts = []
    for p in range(HEAD_PAIRS):
        blk = t[:, p * LANES:(p + 1) * LANES]
        sq = blk * blk
        ms_low = jnp.sum(jnp.where(low, sq, 0.0), axis=-1, keepdims=True)
        ms_high = jnp.sum(jnp.where(low, 0.0, sq), axis=-1, keepdims=True)
        ms = jnp.where(low, ms_low, ms_high) * (1.0 / HEAD_DIM)
        parts.append(blk * lax.rsqrt(ms + EPS))
    return jnp.concatenate(parts, axis=-1) * gain


def _build_bias_table(diag_ref, bias_buf):
    r = lax.broadcasted_iota(jnp.int32, (Q_BLOCK, BIAS_PERIOD), 0)
    j = lax.broadcasted_iota(jnp.int32, (Q_BLOCK, BIAS_PERIOD), 1)
    rel_chunk = r // CHUNK + LEFT_CHUNKS - j // CHUNK
    in_band = jnp.logical_and(rel_chunk >= 0, rel_chunk <= LEFT_CHUNKS)
    for head in range(N_HEADS):
        diag = jnp.broadcast_to(diag_ref[head:head + 1, :], (Q_BLOCK, BIAS_PERIOD))
        table = pltpu.roll(diag, 0, 1, stride=1, stride_axis=0)
        table = jnp.where(in_band, table * LOG2_E, MASK_VALUE)
        row0 = (head % 2) * Q_BLOCK
        bias_buf[head // 2, row0:row0 + Q_BLOCK, :] = table[:, :KEY_WINDOW]


def _mix_kernel(x_ref, mod_ref, g_ref, win_ref, qg_ref, kg_ref, diag_ref, wa_ref, wg_ref,
                ps_ref, wp_ref, wo_ref, o_ref, k_buf, v_buf, u_buf, a_buf, bias_buf):
    tile = pl.program_id(1)
    first = tile == 0

    @pl.when(jnp.logical_and(pl.program_id(0) == 0, first))
    def _():
        _build_bias_table(diag_ref, bias_buf)

    @pl.when(first)
    def _():
        k_buf[0:ROW_TILE, :] = jnp.zeros((ROW_TILE, ATTN_WIDTH), k_buf.dtype)
        v_buf[0:ROW_TILE, :] = jnp.zeros((ROW_TILE, ATTN_WIDTH), v_buf.dtype)
        u_buf[0:POOL_HISTORY, :] = jnp.zeros((POOL_HISTORY, POOL_WIDTH), u_buf.dtype)

    @pl.when(jnp.logical_not(first))
    def _():
        k_buf[0:ROW_TILE, :] = k_buf[ROW_TILE:2 * ROW_TILE, :]
        v_buf[0:ROW_TILE, :] = v_buf[ROW_TILE:2 * ROW_TILE, :]
        u_buf[0:POOL_HISTORY, :] = u_buf[ROW_TILE:ROW_TILE + POOL_HISTORY, :]

    x = x_ref[...]
    shift = mod_ref[0, 3:4, :]
    scale = mod_ref[0, 4:5, :]
    gate = mod_ref[0, 5:6, :]
    h = _norm_modulate(x, g_ref[...], shift, scale)

    def proj(col0, width):
        return jnp.dot(h, win_ref[:, col0:col0 + width], preferred_element_type=jnp.float32)

    q = proj(0, ATTN_WIDTH)
    k = proj(ATTN_WIDTH, ATTN_WIDTH)
    v = proj(2 * ATTN_WIDTH, ATTN_WIDTH)
    u = proj(3 * ATTN_WIDTH, POOL_WIDTH)

    qn = (_head_rms_norm(q, qg_ref[...]) * (HEAD_DIM ** -0.5 * LOG2_E)).astype(jnp.bfloat16)
    k_buf[ROW_TILE:2 * ROW_TILE, :] = _head_rms_norm(k, kg_ref[...]).astype(jnp.bfloat16)
    v_buf[ROW_TILE:2 * ROW_TILE, :] = v.astype(jnp.bfloat16)

    lane = lax.broadcasted_iota(jnp.int32, (Q_BLOCK, LANES), 1)
    low = lane < HEAD_DIM
    key_col = lax.broadcasted_iota(jnp.int32, (1, KEY_WINDOW), 1)
    for half in range(ROW_TILE // QK_ROWS):
        base = half * QK_ROWS
        for p in range(HEAD_PAIRS):
            cols = slice(p * LANES, (p + 1) * LANES)
            stack = []
            for blk in range(QK_ROWS // Q_BLOCK):
                q_pair = qn[base + blk * Q_BLOCK:base + (blk + 1) * Q_BLOCK, cols]
                zero = jnp.zeros_like(q_pair)
                stack += [jnp.where(low, q_pair, zero), jnp.where(low, zero, q_pair)]
            s_all = lax.dot_general(jnp.concatenate(stack, axis=0), k_buf[base:base + QK_SPAN, cols],
                                    (((1,), (1,)), ((), ())),
                                    preferred_element_type=jnp.float32)
            for blk in range(QK_ROWS // Q_BLOCK):
                row0 = base + blk * Q_BLOCK
                missing = jnp.logical_and(first, key_col < ROW_TILE - row0)
                col_mask = jnp.where(missing, MASK_VALUE, 0.0)
                s = s_all[2 * blk * Q_BLOCK:2 * (blk + 1) * Q_BLOCK,
                          blk * Q_BLOCK:blk * Q_BLOCK + KEY_WINDOW]
                s = s + bias_buf[p] + col_mask
                m = jnp.max(s, axis=-1, keepdims=True)
                pexp = jnp.exp2(s - m)
                denom = jnp.sum(pexp, axis=-1, keepdims=True)
                o = jnp.dot(pexp.astype(jnp.bfloat16), v_buf[row0:row0 + KEY_WINDOW, cols],
                            preferred_element_type=jnp.float32) / denom
                a_buf[row0:row0 + Q_BLOCK, cols] = jnp.where(
                    low, o[:Q_BLOCK], o[Q_BLOCK:]).astype(a_buf.dtype)
    ya = jnp.dot(a_buf[...], wa_ref[...], preferred_element_type=jnp.float32)

    u_buf[POOL_HISTORY:POOL_HISTORY + ROW_TILE, :] = u
    pos = tile * ROW_TILE + lax.broadcasted_iota(jnp.int32, (ROW_TILE, 1), 0)
    pooled = []
    for gi, window in enumerate(POOL_WINDOWS):
        cols = slice(gi * POOL_GROUP_DIM, (gi + 1) * POOL_GROUP_DIM)
        total = None
        for j in range(window):
            term = u_buf[POOL_HISTORY - j:POOL_HISTORY - j + ROW_TILE, cols]
            total = term if total is None else total + term
        count = jnp.minimum(pos + 1, window).astype(jnp.float32)
        mixed = (total / count - u[:, cols]).astype(jnp.bfloat16)
        y = jnp.dot(mixed, wg_ref[gi], preferred_element_type=jnp.float32)
        pooled.append((y * ps_ref[:, cols]).astype(jnp.bfloat16))
    yb = jnp.dot(jnp.concatenate(pooled, axis=-1), wp_ref[...], preferred_element_type=jnp.float32)

    ga = proj(3 * ATTN_WIDTH + POOL_WIDTH, D_MODEL)
    gb = proj(3 * ATTN_WIDTH + POOL_WIDTH + D_MODEL, D_MODEL)
    merged = (jax.nn.sigmoid(ga) * ya + jax.nn.sigmoid(gb) * yb).astype(jnp.bfloat16)
    o_ref[...] = x + gate * jnp.dot(merged, wo_ref[...], preferred_element_type=jnp.float32)


def _bias_diagonals(rel_bias):
    n_far = BAND - REL_CLIP + 1
    n_near = KEY_WINDOW - n_far
    far = rel_bias[:, 2 * REL_CLIP:]
    near = rel_bias[:, 2 * REL_CLIP - n_near:2 * REL_CLIP][:, ::-1]
    return jnp.concatenate(
        [jnp.broadcast_to(far, (N_HEADS, n_far)), near,
         jnp.broadcast_to(far, (N_HEADS, BIAS_PERIOD - KEY_WINDOW))], axis=1).astype(jnp.float32)


def _token_mix(x3d, mod, g, w_in, q_gain, k_gain, rel_bias, w_attn_out, w_pool_group, pool_scale,
               w_pool_out, w_o):
    n_batch, seq, _ = x3d.shape
    bf16 = jnp.bfloat16
    in_cols = w_in.shape[1]
    return pl.pallas_call(
        _mix_kernel,
        grid=(n_batch, seq // ROW_TILE),
        in_specs=[
            pl.BlockSpec((None, ROW_TILE, D_MODEL), lambda b, i: (b, i, 0)),
            pl.BlockSpec((1, N_ADA, D_MODEL), lambda b, i: (b, 0, 0)),
            _resident((1, D_MODEL)),
            _resident((D_MODEL, in_cols)),
            _resident((1, ATTN_WIDTH)),
            _resident((1, ATTN_WIDTH)),
            _resident((N_HEADS, BIAS_PERIOD)),
            _resident((ATTN_WIDTH, D_MODEL)),
            _resident((POOL_GROUPS, POOL_GROUP_DIM, POOL_GROUP_DIM)),
            _resident((1, POOL_WIDTH)),
            _resident((POOL_WIDTH, D_MODEL)),
            _resident((D_MODEL, D_MODEL)),
        ],
        out_specs=pl.BlockSpec((None, ROW_TILE, D_MODEL), lambda b, i: (b, i, 0)),
        out_shape=jax.ShapeDtypeStruct(x3d.shape, x3d.dtype),
        scratch_shapes=[
            pltpu.VMEM((2 * ROW_TILE, ATTN_WIDTH), bf16),
            pltpu.VMEM((2 * ROW_TILE, ATTN_WIDTH), bf16),
            pltpu.VMEM((POOL_HISTORY + ROW_TILE, POOL_WIDTH), jnp.float32),
            pltpu.VMEM((ROW_TILE, ATTN_WIDTH), bf16),
            pltpu.VMEM((HEAD_PAIRS, 2 * Q_BLOCK, KEY_WINDOW), jnp.float32),
        ],
        compiler_params=pltpu.CompilerParams(
            dimension_semantics=("arbitrary", "arbitrary"), vmem_limit_bytes=VMEM_LIMIT_BYTES),
        name="token_mix_sublayer",
    )(x3d, mod, g.reshape(1, D_MODEL), w_in.astype(bf16),
      jnp.tile(q_gain, N_HEADS).reshape(1, ATTN_WIDTH), jnp.tile(k_gain, N_HEADS).reshape(1, ATTN_WIDTH),
      _bias_diagonals(rel_bias), w_attn_out.astype(bf16), w_pool_group.astype(bf16),
      pool_scale.reshape(1, POOL_WIDTH), w_pool_out.astype(bf16), w_o.astype(bf16))


def kernel(x, c, w_ada, b_ada, g_ffn1, w_ffn1_in, w_ffn1_out, g_mix, w_in, q_gain, k_gain, rel_bias,
           w_attn_out, w_pool_group, pool_scale, w_pool_out, w_o, g_ffn2, w_ffn2_in, w_ffn2_out):
    n_batch, seq, d_model = x.shape
    depth = w_ada.shape[0]
    assert d_model == D_MODEL and seq % ROW_TILE == 0
    tiles_per_batch = seq // ROW_TILE
    for l in range(depth):
        mod = _modulation(c, w_ada[l], b_ada[l]).reshape(n_batch, N_ADA, D_MODEL)
        x2d = x.reshape(n_batch * seq, D_MODEL)
        x2d = _ffn(x2d, mod, g_ffn1[l], w_ffn1_in[l], w_ffn1_out[l], mod_row=0,
                   tiles_per_batch=tiles_per_batch)
        x = _token_mix(x2d.reshape(n_batch, seq, D_MODEL), mod, g_mix[l], w_in[l], q_gain[l], k_gain[l],
                       rel_bias[l], w_attn_out[l], w_pool_group[l], pool_scale[l], w_pool_out[l], w_o[l])
        x2d = _ffn(x.reshape(n_batch * seq, D_MODEL), mod, g_ffn2[l], w_ffn2_in[l], w_ffn2_out[l],
                   mod_row=6, tiles_per_batch=tiles_per_batch)
        x = x2d.reshape(n_batch, seq, D_MODEL)
    return x
```

```python
import functools
import math

import jax
import jax.numpy as jnp
from jax import lax
from jax.experimental import pallas as pl
from jax.experimental.pallas import tpu as pltpu

D_MODEL = 1024
CHUNK = 64
LEFT_CHUNKS = 8
N_HEADS = 8
HEAD_DIM = 64
ATTN_WIDTH = N_HEADS * HEAD_DIM
POOL_WINDOWS = (2, 4, 8, 16)
POOL_GROUPS = len(POOL_WINDOWS)
POOL_WIDTH = 512
POOL_GROUP_DIM = POOL_WIDTH // POOL_GROUPS
REL_CLIP = 128
D_FF = 2816
N_ADA = 9
EPS = 1e-6
MASK_VALUE = -1e30
LOG2_E = math.log2(math.e)

LANES = 128
MXU_DIM = 256
VMEM_LIMIT_BYTES = 56 * 1024 * 1024

ROW_TILE = 512
FF_CHUNK = MXU_DIM
N_FF_CHUNKS = D_FF // FF_CHUNK
BAND = LEFT_CHUNKS * CHUNK
Q_BLOCK = 2 * CHUNK
KEY_WINDOW = Q_BLOCK + BAND
QK_ROWS = 2 * Q_BLOCK
QK_SPAN = QK_ROWS + BAND
HEAD_PAIRS = ATTN_WIDTH // LANES
V_GROUP = 2 * LANES
BIAS_PERIOD = 768
POOL_HISTORY = 16
MOD_COLS = 1152

assert D_FF % FF_CHUNK == 0
assert ROW_TILE == BAND
assert ROW_TILE % QK_ROWS == 0 and 2 * HEAD_DIM == LANES
assert BIAS_PERIOD >= Q_BLOCK + KEY_WINDOW - 1 and BIAS_PERIOD % LANES == 0
assert POOL_HISTORY >= max(POOL_WINDOWS) - 1 and POOL_HISTORY % 8 == 0
assert (N_ADA * D_MODEL) % MOD_COLS == 0 and MOD_COLS % LANES == 0


def _resident(shape):
    zeros = (0,) * len(shape)
    return pl.BlockSpec(shape, lambda *_: zeros, pipeline_mode=pl.Buffered(1))


def _mod_kernel(ct_ref, w_ref, b_ref, o_ref):
    n_batch = ct_ref.shape[1]
    k_rows = 256
    for b in range(n_batch):
        acc = b_ref[...]
        for k0 in range(0, D_MODEL, k_rows):
            col = ct_ref[k0:k0 + k_rows, b:b + 1]
            col = col * jax.nn.sigmoid(col)
            acc = acc + jnp.sum(w_ref[k0:k0 + k_rows, :] * col, axis=0, keepdims=True)
        o_ref[b:b + 1, :] = acc


def _modulation(c, w_ada, b_ada):
    n_batch = c.shape[0]
    n_cols = w_ada.shape[1]
    return pl.pallas_call(
        _mod_kernel,
        grid=(n_cols // MOD_COLS,),
        in_specs=[
            pl.BlockSpec((D_MODEL, n_batch), lambda j: (0, 0)),
            pl.BlockSpec((D_MODEL, MOD_COLS), lambda j: (0, j)),
            pl.BlockSpec((1, MOD_COLS), lambda j: (0, j)),
        ],
        out_specs=pl.BlockSpec((n_batch, MOD_COLS), lambda j: (0, j)),
        out_shape=jax.ShapeDtypeStruct((n_batch, n_cols), jnp.float32),
        compiler_params=pltpu.CompilerParams(
            dimension_semantics=("arbitrary",), vmem_limit_bytes=VMEM_LIMIT_BYTES),
        name="adaln_modulation",
    )(c.T, w_ada, b_ada.reshape(1, n_cols))


def _norm_modulate(x, g, shift, scale):
    ms = jnp.mean(x * x, axis=-1, keepdims=True)
    y = x * lax.rsqrt(ms + EPS) * g
    return (y * (1.0 + scale) + shift).astype(jnp.bfloat16)


def _ffn_kernel(x_ref, mod_ref, g_ref, win_ref, wout_ref, o_ref, *, mod_row):
    x = x_ref[...]
    shift = mod_ref[0, mod_row:mod_row + 1, :]
    scale = mod_ref[0, mod_row + 1:mod_row + 2, :]
    gate = mod_ref[0, mod_row + 2:mod_row + 3, :]
    h = _norm_modulate(x, g_ref[...], shift, scale)
    acc = None
    for c in range(N_FF_CHUNKS):
        lo = c * FF_CHUNK
        a = jnp.dot(h, win_ref[:, lo:lo + FF_CHUNK], preferred_element_type=jnp.float32)
        b = jnp.dot(h, win_ref[:, D_FF + lo:D_FF + lo + FF_CHUNK], preferred_element_type=jnp.float32)
        act = (a * jax.nn.sigmoid(a) * b).astype(jnp.bfloat16)
        part = jnp.dot(act, wout_ref[lo:lo + FF_CHUNK, :], preferred_element_type=jnp.float32)
        acc = part if acc is None else acc + part
    o_ref[...] = x + (0.5 * gate) * acc


def _ffn(x2d, mod, g, w_in, w_out, *, mod_row, tiles_per_batch):
    n_rows = x2d.shape[0]
    return pl.pallas_call(
        functools.partial(_ffn_kernel, mod_row=mod_row),
        grid=(n_rows // ROW_TILE,),
        in_specs=[
            pl.BlockSpec((ROW_TILE, D_MODEL), lambda i: (i, 0)),
            pl.BlockSpec((1, N_ADA, D_MODEL), lambda i: (i // tiles_per_batch, 0, 0)),
            _resident((1, D_MODEL)),
            _resident((D_MODEL, 2 * D_FF)),
            _resident((D_FF, D_MODEL)),
        ],
        out_specs=pl.BlockSpec((ROW_TILE, D_MODEL), lambda i: (i, 0)),
        out_shape=jax.ShapeDtypeStruct(x2d.shape, x2d.dtype),
        compiler_params=pltpu.CompilerParams(
            dimension_semantics=("arbitrary",), vmem_limit_bytes=VMEM_LIMIT_BYTES),
        name="swiglu_sublayer",
    )(x2d, mod, g.reshape(1, D_MODEL), w_in.astype(jnp.bfloat16), w_out.astype(jnp.bfloat16))


def _head_rms_norm(t, gain):
    lane = lax.broadcasted_iota(jnp.int32, (t.shape[0], LANES), 1)
    low = lane < HEAD_DIM
    parts = []
    for p in range(HEAD_PAIRS):
        blk = t[:, p * LANES:(p + 1) * LANES]
        sq = blk * blk
        ms_low = jnp.sum(jnp.where(low, sq, 0.0), axis=-1, keepdims=True)
        ms_high = jnp.sum(jnp.where(low, 0.0, sq), axis=-1, keepdims=True)
        ms = jnp.where(low, ms_low, ms_high) * (1.0 / HEAD_DIM)
        parts.append(blk * lax.rsqrt(ms + EPS))
    return jnp.concatenate(parts, axis=-1) * gain


def _build_bias_table(diag_ref, bias_buf):
    r = lax.broadcasted_iota(jnp.int32, (Q_BLOCK, BIAS_PERIOD), 0)
    j = lax.broadcasted_iota(jnp.int32, (Q_BLOCK, BIAS_PERIOD), 1)
    rel_chunk = r // CHUNK + LEFT_CHUNKS - j // CHUNK
    in_band = jnp.logical_and(rel_chunk >= 0, rel_chunk <= LEFT_CHUNKS)
    for head in range(N_HEADS):
        diag = jnp.broadcast_to(diag_ref[head:head + 1, :], (Q_BLOCK, BIAS_PERIOD))
        table = pltpu.roll(diag, 0, 1, stride=1, stride_axis=0)
        table = jnp.where(in_band, table * LOG2_E, MASK_VALUE)
        row0 = (head % 2) * Q_BLOCK
        bias_buf[head // 2, row0:row0 + Q_BLOCK, :] = table[:, :KEY_WINDOW]
    bias_buf[HEAD_PAIRS:2 * HEAD_PAIRS] = jnp.full((HEAD_PAIRS, 2 * Q_BLOCK, KEY_WINDOW), MASK_VALUE,
                                                   bias_buf.dtype)


def _mix_kernel(x_ref, mod_ref, g_ref, win_ref, qg_ref, kg_ref, diag_ref, wa_ref, wg_ref,
                ps_ref, wp_ref, wo_ref, o_ref, k_buf, v_buf, u_buf, a_buf, bias_buf):
    tile = pl.program_id(1)
    first = tile == 0

    @pl.when(jnp.logical_and(pl.program_id(0) == 0, first))
    def _():
        _build_bias_table(diag_ref, bias_buf)
        for p in range(HEAD_PAIRS):
            v_buf[ROW_TILE:2 * ROW_TILE, p * V_GROUP + LANES:(p + 1) * V_GROUP] = jnp.ones(
                (ROW_TILE, LANES), v_buf.dtype)

    @pl.when(first)
    def _():
        k_buf[0:ROW_TILE, :] = jnp.zeros((ROW_TILE, ATTN_WIDTH), k_buf.dtype)
        v_buf[0:ROW_TILE, :] = jnp.zeros((ROW_TILE, HEAD_PAIRS * V_GROUP), v_buf.dtype)
        u_buf[0:POOL_HISTORY, :] = jnp.zeros((POOL_HISTORY, POOL_WIDTH), u_buf.dtype)

    @pl.when(jnp.logical_not(first))
    def _():
        k_buf[0:ROW_TILE, :] = k_buf[ROW_TILE:2 * ROW_TILE, :]
        v_buf[0:ROW_TILE, :] = v_buf[ROW_TILE:2 * ROW_TILE, :]
        u_buf[0:POOL_HISTORY, :] = u_buf[ROW_TILE:ROW_TILE + POOL_HISTORY, :]

    x = x_ref[...]
    shift = mod_ref[0, 3:4, :]
    scale = mod_ref[0, 4:5, :]
    gate = mod_ref[0, 5:6, :]
    h = _norm_modulate(x, g_ref[...], shift, scale)

    def proj(col0, width):
        return jnp.dot(h, win_ref[:, col0:col0 + width], preferred_element_type=jnp.float32)

    q = proj(0, ATTN_WIDTH)
    k = proj(ATTN_WIDTH, ATTN_WIDTH)
    v = proj(2 * ATTN_WIDTH, ATTN_WIDTH)
    u = proj(3 * ATTN_WIDTH, POOL_WIDTH)

    qn = (_head_rms_norm(q, qg_ref[...]) * (HEAD_DIM ** -0.5 * LOG2_E)).astype(jnp.bfloat16)
    k_buf[ROW_TILE:2 * ROW_TILE, :] = _head_rms_norm(k, kg_ref[...]).astype(jnp.bfloat16)
    for p in range(HEAD_PAIRS):
        v_buf[ROW_TILE:2 * ROW_TILE, p * V_GROUP:p * V_GROUP + LANES] = (
            v[:, p * LANES:(p + 1) * LANES].astype(jnp.bfloat16))

    lane = lax.broadcasted_iota(jnp.int32, (Q_BLOCK, LANES), 1)
    low = lane < HEAD_DIM
    masked_table = first.astype(jnp.int32) * HEAD_PAIRS

    def scores(half, p):
        base = half * QK_ROWS
        cols = slice(p * LANES, (p + 1) * LANES)
        stack = []
        for blk in range(QK_ROWS // Q_BLOCK):
            q_pair = qn[base + blk * Q_BLOCK:base + (blk + 1) * Q_BLOCK, cols]
            zero = jnp.zeros_like(q_pair)
            stack += [jnp.where(low, q_pair, zero), jnp.where(low, zero, q_pair)]
        return lax.dot_general(jnp.concatenate(stack, axis=0), k_buf[base:base + QK_SPAN, cols],
                               (((1,), (1,)), ((), ())),
                               preferred_element_type=jnp.float32)

    def attend(half, p, s_all):
        cols = slice(p * LANES, (p + 1) * LANES)
        for blk in range(QK_ROWS // Q_BLOCK):
            row0 = half * QK_ROWS + blk * Q_BLOCK
            bias = jnp.concatenate(
                [bias_buf[(masked_table if c0 < ROW_TILE - row0 else 0) + p, :, c0:c0 + LANES]
                 for c0 in range(0, KEY_WINDOW, LANES)], axis=1)
            s = s_all[2 * blk * Q_BLOCK:2 * (blk + 1) * Q_BLOCK,
                      blk * Q_BLOCK:blk * Q_BLOCK + KEY_WINDOW] + bias
            m = jnp.max(s, axis=-1, keepdims=True)
            pexp = jnp.exp2((s - m).astype(jnp.bfloat16))
            o = jnp.dot(pexp, v_buf[row0:row0 + KEY_WINDOW, p * V_GROUP:(p + 1) * V_GROUP],
                        preferred_element_type=jnp.float32)
            o = o[:, :LANES] / o[:, LANES:]
            a_buf[row0:row0 + Q_BLOCK, cols] = jnp.where(
                low, o[:Q_BLOCK], o[Q_BLOCK:]).astype(a_buf.dtype)

    units = [(half, p) for half in range(ROW_TILE // QK_ROWS) for p in range(HEAD_PAIRS)]
    gate_col0 = 3 * ATTN_WIDTH + POOL_WIDTH
    gate_width = 2 * D_MODEL // len(units)
    gate_parts = []
    s_next = scores(*units[0])
    for n, unit in enumerate(units):
        s_cur = s_next
        if n + 1 < len(units):
            s_next = scores(*units[n + 1])
        gate_parts.append(proj(gate_col0 + n * gate_width, gate_width))
        attend(*unit, s_cur)
    gates = jnp.concatenate(gate_parts, axis=1)
    ya = jnp.dot(a_buf[...], wa_ref[...], preferred_element_type=jnp.float32)

    u_buf[POOL_HISTORY:POOL_HISTORY + ROW_TILE, :] = u
    pos = tile * ROW_TILE + lax.broadcasted_iota(jnp.int32, (ROW_TILE, 1), 0)
    pooled = []
    for gi, window in enumerate(POOL_WINDOWS):
        cols = slice(gi * POOL_GROUP_DIM, (gi + 1) * POOL_GROUP_DIM)
        total = None
        for j in range(window):
            term = u_buf[POOL_HISTORY - j:POOL_HISTORY - j + ROW_TILE, cols]
            total = term if total is None else total + term
        count = jnp.minimum(pos + 1, window).astype(jnp.float32)
        mixed = (total / count - u[:, cols]).astype(jnp.bfloat16)
        y = jnp.dot(mixed, wg_ref[gi], preferred_element_type=jnp.float32)
        pooled.append((y * ps_ref[:, cols]).astype(jnp.bfloat16))
    yb = jnp.dot(jnp.concatenate(pooled, axis=-1), wp_ref[...], preferred_element_type=jnp.float32)

    merged = (jax.nn.sigmoid(gates[:, :D_MODEL]) * ya
              + jax.nn.sigmoid(gates[:, D_MODEL:]) * yb).astype(jnp.bfloat16)
    o_ref[...] = x + gate * jnp.dot(merged, wo_ref[...], preferred_element_type=jnp.float32)


def _bias_diagonals(rel_bias):
    n_far = BAND - REL_CLIP + 1
    n_near = KEY_WINDOW - n_far
    far = rel_bias[:, 2 * REL_CLIP:]
    near = rel_bias[:, 2 * REL_CLIP - n_near:2 * REL_CLIP][:, ::-1]
    return jnp.concatenate(
        [jnp.broadcast_to(far, (N_HEADS, n_far)), near,
         jnp.broadcast_to(far, (N_HEADS, BIAS_PERIOD - KEY_WINDOW))], axis=1).astype(jnp.float32)


def _token_mix(x3d, mod, g, w_in, q_gain, k_gain, rel_bias, w_attn_out, w_pool_group, pool_scale,
               w_pool_out, w_o):
    n_batch, seq, _ = x3d.shape
    bf16 = jnp.bfloat16
    in_cols = w_in.shape[1]
    return pl.pallas_call(
        _mix_kernel,
        grid=(n_batch, seq // ROW_TILE),
        in_specs=[
            pl.BlockSpec((None, ROW_TILE, D_MODEL), lambda b, i: (b, i, 0)),
            pl.BlockSpec((1, N_ADA, D_MODEL), lambda b, i: (b, 0, 0)),
            _resident((1, D_MODEL)),
            _resident((D_MODEL, in_cols)),
            _resident((1, ATTN_WIDTH)),
            _resident((1, ATTN_WIDTH)),
            _resident((N_HEADS, BIAS_PERIOD)),
            _resident((ATTN_WIDTH, D_MODEL)),
            _resident((POOL_GROUPS, POOL_GROUP_DIM, POOL_GROUP_DIM)),
            _resident((1, POOL_WIDTH)),
            _resident((POOL_WIDTH, D_MODEL)),
            _resident((D_MODEL, D_MODEL)),
        ],
        out_specs=pl.BlockSpec((None, ROW_TILE, D_MODEL), lambda b, i: (b, i, 0)),
        out_shape=jax.ShapeDtypeStruct(x3d.shape, x3d.dtype),
        scratch_shapes=[
            pltpu.VMEM((2 * ROW_TILE, ATTN_WIDTH), bf16),
            pltpu.VMEM((2 * ROW_TILE, HEAD_PAIRS * V_GROUP), bf16),
            pltpu.VMEM((POOL_HISTORY + ROW_TILE, POOL_WIDTH), jnp.float32),
            pltpu.VMEM((ROW_TILE, ATTN_WIDTH), bf16),
            pltpu.VMEM((2 * HEAD_PAIRS, 2 * Q_BLOCK, KEY_WINDOW), jnp.float32),
        ],
        compiler_params=pltpu.CompilerParams(
            dimension_semantics=("arbitrary", "arbitrary"), vmem_limit_bytes=VMEM_LIMIT_BYTES),
        name="token_mix_sublayer",
    )(x3d, mod, g.reshape(1, D_MODEL), w_in.astype(bf16),
      jnp.tile(q_gain, N_HEADS).reshape(1, ATTN_WIDTH), jnp.tile(k_gain, N_HEADS).reshape(1, ATTN_WIDTH),
      _bias_diagonals(rel_bias), w_attn_out.astype(bf16), w_pool_group.astype(bf16),
      pool_scale.reshape(1, POOL_WIDTH), w_pool_out.astype(bf16), w_o.astype(bf16))


def kernel(x, c, w_ada, b_ada, g_ffn1, w_ffn1_in, w_ffn1_out, g_mix, w_in, q_gain, k_gain, rel_bias,
           w_attn_out, w_pool_group, pool_scale, w_pool_out, w_o, g_ffn2, w_ffn2_in, w_ffn2_out):
    n_batch, seq, d_model = x.shape
    depth = w_ada.shape[0]
    assert d_model == D_MODEL and seq % ROW_TILE == 0
    tiles_per_batch = seq // ROW_TILE
    for l in range(depth):
        mod = _modulation(c, w_ada[l], b_ada[l]).reshape(n_batch, N_ADA, D_MODEL)
        x2d = x.reshape(n_batch * seq, D_MODEL)
        x2d = _ffn(x2d, mod, g_ffn1[l], w_ffn1_in[l], w_ffn1_out[l], mod_row=0,
                   tiles_per_batch=tiles_per_batch)
        x = _token_mix(x2d.reshape(n_batch, seq, D_MODEL), mod, g_mix[l], w_in[l], q_gain[l], k_gain[l],
                       rel_bias[l], w_attn_out[l], w_pool_group[l], pool_scale[l], w_pool_out[l], w_o[l])
        x2d = _ffn(x.reshape(n_batch * seq, D_MODEL), mod, g_ffn2[l], w_ffn2_in[l], w_ffn2_out[l],
                   mod_row=6, tiles_per_batch=tiles_per_batch)
        x = x2d.reshape(n_batch, seq, D_MODEL)
    return x
```

```python
import functools
import math

import jax
import jax.numpy as jnp
from jax import lax
from jax.experimental import pallas as pl
from jax.experimental.pallas import tpu as pltpu

D_MODEL = 1024
CHUNK = 64
LEFT_CHUNKS = 8
N_HEADS = 8
HEAD_DIM = 64
ATTN_WIDTH = N_HEADS * HEAD_DIM
POOL_WINDOWS = (2, 4, 8, 16)
POOL_GROUPS = len(POOL_WINDOWS)
POOL_WIDTH = 512
POOL_GROUP_DIM = POOL_WIDTH // POOL_GROUPS
REL_CLIP = 128
D_FF = 2816
N_ADA = 9
EPS = 1e-6
MASK_VALUE = -1e30
LOG2_E = math.log2(math.e)

LANES = 128
MXU_DIM = 256
VMEM_LIMIT_BYTES = 56 * 1024 * 1024

ROW_TILE = 512
FFN_ROW_TILE = 1024
FF_CHUNK = MXU_DIM
N_FF_CHUNKS = D_FF // FF_CHUNK
BAND = LEFT_CHUNKS * CHUNK
Q_BLOCK = 2 * CHUNK
KEY_WINDOW = Q_BLOCK + BAND
QK_ROWS = 2 * Q_BLOCK
QK_SPAN = QK_ROWS + BAND
HEAD_PAIRS = ATTN_WIDTH // LANES
V_GROUP = 2 * LANES
BIAS_PERIOD = 768
POOL_HISTORY = 16
MOD_COLS = 1152

assert D_FF % FF_CHUNK == 0
assert ROW_TILE == BAND
assert ROW_TILE % QK_ROWS == 0 and 2 * HEAD_DIM == LANES
assert BIAS_PERIOD >= Q_BLOCK + KEY_WINDOW - 1 and BIAS_PERIOD % LANES == 0
assert POOL_HISTORY >= max(POOL_WINDOWS) - 1 and POOL_HISTORY % 8 == 0
assert (N_ADA * D_MODEL) % MOD_COLS == 0 and MOD_COLS % LANES == 0


def _resident(shape):
    zeros = (0,) * len(shape)
    return pl.BlockSpec(shape, lambda *_: zeros, pipeline_mode=pl.Buffered(1))


def _mod_kernel(ct_ref, w_ref, b_ref, o_ref):
    n_batch = ct_ref.shape[1]
    k_rows = 256
    for b in range(n_batch):
        acc = b_ref[...]
        for k0 in range(0, D_MODEL, k_rows):
            col = ct_ref[k0:k0 + k_rows, b:b + 1]
            col = col * jax.nn.sigmoid(col)
            acc = acc + jnp.sum(w_ref[k0:k0 + k_rows, :] * col, axis=0, keepdims=True)
        o_ref[b:b + 1, :] = acc


def _modulation(c, w_ada, b_ada):
    n_batch = c.shape[0]
    n_cols = w_ada.shape[1]
    return pl.pallas_call(
        _mod_kernel,
        grid=(n_cols // MOD_COLS,),
        in_specs=[
            pl.BlockSpec((D_MODEL, n_batch), lambda j: (0, 0)),
            pl.BlockSpec((D_MODEL, MOD_COLS), lambda j: (0, j)),
            pl.BlockSpec((1, MOD_COLS), lambda j: (0, j)),
        ],
        out_specs=pl.BlockSpec((n_batch, MOD_COLS), lambda j: (0, j)),
        out_shape=jax.ShapeDtypeStruct((n_batch, n_cols), jnp.float32),
        compiler_params=pltpu.CompilerParams(
            dimension_semantics=("arbitrary",), vmem_limit_bytes=VMEM_LIMIT_BYTES),
        name="adaln_modulation",
    )(c.T, w_ada, b_ada.reshape(1, n_cols))


def _norm_modulate(x, g, shift, scale):
    ms = jnp.mean(x * x, axis=-1, keepdims=True)
    col_gain = g * (1.0 + scale)
    return (x * lax.rsqrt(ms + EPS) * col_gain + shift).astype(jnp.bfloat16)


def _ffn_kernel(x_ref, mod_ref, g_ref, win_ref, wout_ref, o_ref, *, mod_row):
    x = x_ref[...]
    shift = mod_ref[0, mod_row:mod_row + 1, :]
    scale = mod_ref[0, mod_row + 1:mod_row + 2, :]
    gate = mod_ref[0, mod_row + 2:mod_row + 3, :]
    h = _norm_modulate(x, g_ref[...], shift, scale)
    acc = None
    for c in range(N_FF_CHUNKS):
        lo = c * FF_CHUNK
        a = jnp.dot(h, win_ref[:, lo:lo + FF_CHUNK], preferred_element_type=jnp.float32)
        b = jnp.dot(h, win_ref[:, D_FF + lo:D_FF + lo + FF_CHUNK], preferred_element_type=jnp.float32)
        act = (a * jax.nn.sigmoid(a) * b).astype(jnp.bfloat16)
        part = jnp.dot(act, wout_ref[lo:lo + FF_CHUNK, :], preferred_element_type=jnp.float32)
        acc = part if acc is None else acc + part
    o_ref[...] = x + (0.5 * gate) * acc


def _ffn(x2d, mod, g, w_in, w_out, *, mod_row, tiles_per_batch):
    n_rows = x2d.shape[0]
    return pl.pallas_call(
        functools.partial(_ffn_kernel, mod_row=mod_row),
        grid=(n_rows // FFN_ROW_TILE,),
        in_specs=[
            pl.BlockSpec((FFN_ROW_TILE, D_MODEL), lambda i: (i, 0)),
            pl.BlockSpec((1, N_ADA, D_MODEL), lambda i: (i // tiles_per_batch, 0, 0)),
            _resident((1, D_MODEL)),
            _resident((D_MODEL, 2 * D_FF)),
            _resident((D_FF, D_MODEL)),
        ],
        out_specs=pl.BlockSpec((FFN_ROW_TILE, D_MODEL), lambda i: (i, 0)),
        out_shape=jax.ShapeDtypeStruct(x2d.shape, x2d.dtype),
        compiler_params=pltpu.CompilerParams(
            dimension_semantics=("arbitrary",), vmem_limit_bytes=VMEM_LIMIT_BYTES),
        name="swiglu_sublayer",
    )(x2d, mod, g.reshape(1, D_MODEL), w_in.astype(jnp.bfloat16), w_out.astype(jnp.bfloat16))


def _head_rms_norm(t, gain):
    lane = lax.broadcasted_iota(jnp.int32, (t.shape[0], LANES), 1)
    low = lane < HEAD_DIM
    parts = []
    for p in range(HEAD_PAIRS):
        blk = t[:, p * LANES:(p + 1) * LANES]
        sq = blk * blk
        ms_low = jnp.sum(jnp.where(low, sq, 0.0), axis=-1, keepdims=True)
        ms_high = jnp.sum(jnp.where(low, 0.0, sq), axis=-1, keepdims=True)
        ms = jnp.where(low, ms_low, ms_high) * (1.0 / HEAD_DIM)
        parts.append(blk * lax.rsqrt(ms + EPS))
    return jnp.concatenate(parts, axis=-1) * gain


def _build_bias_table(diag_ref, bias_buf):
    r = lax.broadcasted_iota(jnp.int32, (Q_BLOCK, BIAS_PERIOD), 0)
    j = lax.broadcasted_iota(jnp.int32, (Q_BLOCK, BIAS_PERIOD), 1)
    rel_chunk = r // CHUNK + LEFT_CHUNKS - j // CHUNK
    in_band = jnp.logical_and(rel_chunk >= 0, rel_chunk <= LEFT_CHUNKS)
    for head in range(N_HEADS):
        diag = jnp.broadcast_to(diag_ref[head:head + 1, :], (Q_BLOCK, BIAS_PERIOD))
        table = pltpu.roll(diag, 0, 1, stride=1, stride_axis=0)
        table = jnp.where(in_band, table * LOG2_E, MASK_VALUE)
        row0 = (head % 2) * Q_BLOCK
        bias_buf[head // 2, row0:row0 + Q_BLOCK, :] = table[:, :KEY_WINDOW]
    bias_buf[HEAD_PAIRS:2 * HEAD_PAIRS] = jnp.full((HEAD_PAIRS, 2 * Q_BLOCK, KEY_WINDOW), MASK_VALUE,
                                                   bias_buf.dtype)


def _mix_kernel(x_ref, mod_ref, g_ref, win_ref, qg_ref, kg_ref, diag_ref, wa_ref, wg_ref,
                ps_ref, wp_ref, wo_ref, o_ref, k_buf, v_buf, u_buf, a_buf, bias_buf):
    tile = pl.program_id(1)
    first = tile == 0

    @pl.when(jnp.logical_and(pl.program_id(0) == 0, first))
    def _():
        _build_bias_table(diag_ref, bias_buf)
        for p in range(HEAD_PAIRS):
            v_buf[ROW_TILE:2 * ROW_TILE, p * V_GROUP + LANES:(p + 1) * V_GROUP] = jnp.ones(
                (ROW_TILE, LANES), v_buf.dtype)

    @pl.when(first)
    def _():
        k_buf[0:ROW_TILE, :] = jnp.zeros((ROW_TILE, ATTN_WIDTH), k_buf.dtype)
        v_buf[0:ROW_TILE, :] = jnp.zeros((ROW_TILE, HEAD_PAIRS * V_GROUP), v_buf.dtype)
        u_buf[0:POOL_HISTORY, :] = jnp.zeros((POOL_HISTORY, POOL_WIDTH), u_buf.dtype)

    x = x_ref[...]
    shift = mod_ref[0, 3:4, :]
    scale = mod_ref[0, 4:5, :]
    gate = mod_ref[0, 5:6, :]
    h = _norm_modulate(x, g_ref[...], shift, scale)

    def proj(col0, width):
        return jnp.dot(h, win_ref[:, col0:col0 + width], preferred_element_type=jnp.float32)

    q = proj(0, ATTN_WIDTH)
    k = proj(ATTN_WIDTH, ATTN_WIDTH)
    v = proj(2 * ATTN_WIDTH, ATTN_WIDTH)
    u = proj(3 * ATTN_WIDTH, POOL_WIDTH)

    qn = (_head_rms_norm(q, qg_ref[...]) * (HEAD_DIM ** -0.5 * LOG2_E)).astype(jnp.bfloat16)
    k_buf[ROW_TILE:2 * ROW_TILE, :] = _head_rms_norm(k, kg_ref[...]).astype(jnp.bfloat16)
    for p in range(HEAD_PAIRS):
        v_buf[ROW_TILE:2 * ROW_TILE, p * V_GROUP:p * V_GROUP + LANES] = (
            v[:, p * LANES:(p + 1) * LANES].astype(jnp.bfloat16))

    lane = lax.broadcasted_iota(jnp.int32, (Q_BLOCK, LANES), 1)
    low = lane < HEAD_DIM
    masked_table = first.astype(jnp.int32) * HEAD_PAIRS

    def scores(half, p):
        base = half * QK_ROWS
        cols = slice(p * LANES, (p + 1) * LANES)
        stack = []
        for blk in range(QK_ROWS // Q_BLOCK):
            q_pair = qn[base + blk * Q_BLOCK:base + (blk + 1) * Q_BLOCK, cols]
            zero = jnp.zeros_like(q_pair)
            stack += [jnp.where(low, q_pair, zero), jnp.where(low, zero, q_pair)]
        return lax.dot_general(jnp.concatenate(stack, axis=0), k_buf[base:base + QK_SPAN, cols],
                               (((1,), (1,)), ((), ())),
                               preferred_element_type=jnp.float32)

    def attend(half, p, s_all):
        cols = slice(p * LANES, (p + 1) * LANES)
        for blk in range(QK_ROWS // Q_BLOCK):
            row0 = half * QK_ROWS + blk * Q_BLOCK
            bias = jnp.concatenate(
                [bias_buf[(masked_table if c0 < ROW_TILE - row0 else 0) + p, :, c0:c0 + LANES]
                 for c0 in range(0, KEY_WINDOW, LANES)], axis=1)
            s = s_all[2 * blk * Q_BLOCK:2 * (blk + 1) * Q_BLOCK,
                      blk * Q_BLOCK:blk * Q_BLOCK + KEY_WINDOW] + bias
            m = jnp.max(s, axis=-1, keepdims=True)
            pexp = jnp.exp2((s - m).astype(jnp.bfloat16))
            o = jnp.dot(pexp, v_buf[row0:row0 + KEY_WINDOW, p * V_GROUP:(p + 1) * V_GROUP],
                        preferred_element_type=jnp.float32)
            o = o[:, :LANES] / o[:, LANES:]
            a_buf[row0:row0 + Q_BLOCK, cols] = jnp.where(
                low, o[:Q_BLOCK], o[Q_BLOCK:]).astype(a_buf.dtype)

    units = [(half, p) for half in range(ROW_TILE // QK_ROWS) for p in range(HEAD_PAIRS)]
    gate_col0 = 3 * ATTN_WIDTH + POOL_WIDTH
    gate_width = 2 * D_MODEL // len(units)
    gate_parts = []
    s_next = scores(*units[0])
    for n, unit in enumerate(units):
        s_cur = s_next
        if n + 1 < len(units):
            s_next = scores(*units[n + 1])
        gate_parts.append(proj(gate_col0 + n * gate_width, gate_width))
        attend(*unit, s_cur)
    gates = jnp.concatenate(gate_parts, axis=1)
    ya = jnp.dot(a_buf[...], wa_ref[...], preferred_element_type=jnp.float32)
    k_buf[0:ROW_TILE, :] = k_buf[ROW_TILE:2 * ROW_TILE, :]
    v_buf[0:ROW_TILE, :] = v_buf[ROW_TILE:2 * ROW_TILE, :]

    u_buf[POOL_HISTORY:POOL_HISTORY + ROW_TILE, :] = u
    pos = tile * ROW_TILE + lax.broadcasted_iota(jnp.int32, (ROW_TILE, 1), 0)
    pooled = []
    for gi, window in enumerate(POOL_WINDOWS):
        cols = slice(gi * POOL_GROUP_DIM, (gi + 1) * POOL_GROUP_DIM)
        total = None
        for j in range(window):
            term = u_buf[POOL_HISTORY - j:POOL_HISTORY - j + ROW_TILE, cols]
            total = term if total is None else total + term
        count = jnp.minimum(pos + 1, window).astype(jnp.float32)
        mixed = (total / count - u[:, cols]).astype(jnp.bfloat16)
        y = jnp.dot(mixed, wg_ref[gi], preferred_element_type=jnp.float32)
        pooled.append((y * ps_ref[:, cols]).astype(jnp.bfloat16))
    yb = jnp.dot(jnp.concatenate(pooled, axis=-1), wp_ref[...], preferred_element_type=jnp.float32)
    u_buf[0:POOL_HISTORY, :] = u_buf[ROW_TILE:ROW_TILE + POOL_HISTORY, :]

    merged = (jax.nn.sigmoid(gates[:, :D_MODEL]) * ya
              + jax.nn.sigmoid(gates[:, D_MODEL:]) * yb).astype(jnp.bfloat16)
    o_ref[...] = x + gate * jnp.dot(merged, wo_ref[...], preferred_element_type=jnp.float32)


def _bias_diagonals(rel_bias):
    n_far = BAND - REL_CLIP + 1
    n_near = KEY_WINDOW - n_far
    far = rel_bias[:, 2 * REL_CLIP:]
    near = rel_bias[:, 2 * REL_CLIP - n_near:2 * REL_CLIP][:, ::-1]
    return jnp.concatenate(
        [jnp.broadcast_to(far, (N_HEADS, n_far)), near,
         jnp.broadcast_to(far, (N_HEADS, BIAS_PERIOD - KEY_WINDOW))], axis=1).astype(jnp.float32)


def _token_mix(x3d, mod, g, w_in, q_gain, k_gain, rel_bias, w_attn_out, w_pool_group, pool_scale,
               w_pool_out, w_o):
    n_batch, seq, _ = x3d.shape
    bf16 = jnp.bfloat16
    in_cols = w_in.shape[1]
    return pl.pallas_call(
        _mix_kernel,
        grid=(n_batch, seq // ROW_TILE),
        in_specs=[
            pl.BlockSpec((None, ROW_TILE, D_MODEL), lambda b, i: (b, i, 0)),
            pl.BlockSpec((1, N_ADA, D_MODEL), lambda b, i: (b, 0, 0)),
            _resident((1, D_MODEL)),
            _resident((D_MODEL, in_cols)),
            _resident((1, ATTN_WIDTH)),
            _resident((1, ATTN_WIDTH)),
            _resident((N_HEADS, BIAS_PERIOD)),
            _resident((ATTN_WIDTH, D_MODEL)),
            _resident((POOL_GROUPS, POOL_GROUP_DIM, POOL_GROUP_DIM)),
            _resident((1, POOL_WIDTH)),
            _resident((POOL_WIDTH, D_MODEL)),
            _resident((D_MODEL, D_MODEL)),
        ],
        out_specs=pl.BlockSpec((None, ROW_TILE, D_MODEL), lambda b, i: (b, i, 0)),
        out_shape=jax.ShapeDtypeStruct(x3d.shape, x3d.dtype),
        scratch_shapes=[
            pltpu.VMEM((2 * ROW_TILE, ATTN_WIDTH), bf16),
            pltpu.VMEM((2 * ROW_TILE, HEAD_PAIRS * V_GROUP), bf16),
            pltpu.VMEM((POOL_HISTORY + ROW_TILE, POOL_WIDTH), jnp.float32),
            pltpu.VMEM((ROW_TILE, ATTN_WIDTH), bf16),
            pltpu.VMEM((2 * HEAD_PAIRS, 2 * Q_BLOCK, KEY_WINDOW), jnp.float32),
        ],
        compiler_params=pltpu.CompilerParams(
            dimension_semantics=("arbitrary", "arbitrary"), vmem_limit_bytes=VMEM_LIMIT_BYTES),
        name="token_mix_sublayer",
    )(x3d, mod, g.reshape(1, D_MODEL), w_in.astype(bf16),
      jnp.tile(q_gain, N_HEADS).reshape(1, ATTN_WIDTH), jnp.tile(k_gain, N_HEADS).reshape(1, ATTN_WIDTH),
      _bias_diagonals(rel_bias), w_attn_out.astype(bf16), w_pool_group.astype(bf16),
      pool_scale.reshape(1, POOL_WIDTH), w_pool_out.astype(bf16), w_o.astype(bf16))


def kernel(x, c, w_ada, b_ada, g_ffn1, w_ffn1_in, w_ffn1_out, g_mix, w_in, q_gain, k_gain, rel_bias,
           w_attn_out, w_pool_group, pool_scale, w_pool_out, w_o, g_ffn2, w_ffn2_in, w_ffn2_out):
    n_batch, seq, d_model = x.shape
    depth = w_ada.shape[0]
    assert d_model == D_MODEL and seq % ROW_TILE == 0 and seq % FFN_ROW_TILE == 0
    tiles_per_batch = seq // FFN_ROW_TILE
    for l in range(depth):
        mod = _modulation(c, w_ada[l], b_ada[l]).reshape(n_batch, N_ADA, D_MODEL)
        x2d = x.reshape(n_batch * seq, D_MODEL)
        x2d = _ffn(x2d, mod, g_ffn1[l], w_ffn1_in[l], w_ffn1_out[l], mod_row=0,
                   tiles_per_batch=tiles_per_batch)
        x = _token_mix(x2d.reshape(n_batch, seq, D_MODEL), mod, g_mix[l], w_in[l], q_gain[l], k_gain[l],
                       rel_bias[l], w_attn_out[l], w_pool_group[l], pool_scale[l], w_pool_out[l], w_o[l])
        x2d = _ffn(x.reshape(n_batch * seq, D_MODEL), mod, g_ffn2[l], w_ffn2_in[l], w_ffn2_out[l],
                   mod_row=6, tiles_per_batch=tiles_per_batch)
        x = x2d.reshape(n_batch, seq, D_MODEL)
    return x
```

```python
import functools
import math

import jax
import jax.numpy as jnp
from jax import lax
from jax.experimental import pallas as pl
from jax.experimental.pallas import tpu as pltpu

D_MODEL = 1024
CHUNK = 64
LEFT_CHUNKS = 8
N_HEADS = 8
HEAD_DIM = 64
ATTN_WIDTH = N_HEADS * HEAD_DIM
POOL_WINDOWS = (2, 4, 8, 16)
POOL_GROUPS = len(POOL_WINDOWS)
POOL_WIDTH = 512
POOL_GROUP_DIM = POOL_WIDTH // POOL_GROUPS
REL_CLIP = 128
D_FF = 2816
N_ADA = 9
EPS = 1e-6
MASK_VALUE = -1e30
LOG2_E = math.log2(math.e)

LANES = 128
MXU_DIM = 256
BF16_SUBLANES = 16
VMEM_LIMIT_BYTES = 56 * 1024 * 1024

ROW_TILE = 512
FFN_ROW_TILE = 1024
FF_CHUNK = MXU_DIM
N_FF_CHUNKS = D_FF // FF_CHUNK
BAND = LEFT_CHUNKS * CHUNK
Q_BLOCK = 2 * CHUNK
KEY_WINDOW = Q_BLOCK + BAND
QK_ROWS = 2 * Q_BLOCK
QK_SPAN = QK_ROWS + BAND
HEAD_PAIRS = ATTN_WIDTH // LANES
V_GROUP = 2 * LANES
BIAS_PERIOD = 768
POOL_HISTORY = 16
MOD_STEPS = 8

assert D_FF % FF_CHUNK == 0
assert ROW_TILE == BAND
assert ROW_TILE % QK_ROWS == 0 and 2 * HEAD_DIM == LANES
assert BIAS_PERIOD >= Q_BLOCK + KEY_WINDOW - 1 and BIAS_PERIOD % LANES == 0
assert POOL_HISTORY >= max(POOL_WINDOWS) - 1 and POOL_HISTORY % 8 == 0
assert D_MODEL % (MOD_STEPS * 8) == 0


def _resident(shape):
    zeros = (0,) * len(shape)
    return pl.BlockSpec(shape, lambda *_: zeros, pipeline_mode=pl.Buffered(1))


def _row_blockable(arr, n_blocks):
    rows, cols = arr.shape
    split = 1
    while (rows * split) % (n_blocks * BF16_SUBLANES) or (cols // split) % LANES:
        split *= 2
        assert cols % split == 0, (arr.shape, n_blocks)
    return arr.reshape(rows * split, cols // split)


def _cast_specs(cast, n_blocks):
    views = [_row_blockable(arr, n_blocks) for arr in cast]
    specs = [pl.BlockSpec((v.shape[0] // n_blocks, v.shape[1]), lambda i: (i, 0)) for v in views]
    return views, specs


def _mod_kernel(ct_ref, w_ref, b_ref, *rest, n_cast):
    cast_src, o_ref, cast_dst = rest[:n_cast], rest[n_cast], rest[n_cast + 1:]

    @pl.when(pl.program_id(0) == 0)
    def _():
        o_ref[...] = jnp.broadcast_to(b_ref[...], o_ref.shape)

    for b in range(ct_ref.shape[1]):
        col = ct_ref[:, b:b + 1]
        col = col * jax.nn.sigmoid(col)
        o_ref[b:b + 1, :] += jnp.sum(w_ref[...] * col, axis=0, keepdims=True)
    for src, dst in zip(cast_src, cast_dst):
        dst[...] = src[...].astype(dst.dtype)


def _modulation(c, w_ada, b_ada, cast=()):
    n_batch = c.shape[0]
    n_cols = w_ada.shape[1]
    rows = D_MODEL // MOD_STEPS
    cast_shapes = [arr.shape for arr in cast]
    cast, cast_specs = _cast_specs(cast, MOD_STEPS)
    out = pl.pallas_call(
        functools.partial(_mod_kernel, n_cast=len(cast)),
        grid=(MOD_STEPS,),
        in_specs=[
            pl.BlockSpec((rows, n_batch), lambda j: (j, 0)),
            pl.BlockSpec((rows, n_cols), lambda j: (j, 0)),
            pl.BlockSpec((1, n_cols), lambda j: (0, 0)),
            *cast_specs,
        ],
        out_specs=[pl.BlockSpec((n_batch, n_cols), lambda j: (0, 0)), *cast_specs],
        out_shape=[jax.ShapeDtypeStruct((n_batch, n_cols), jnp.float32),
                   *[jax.ShapeDtypeStruct(arr.shape, jnp.bfloat16) for arr in cast]],
        compiler_params=pltpu.CompilerParams(
            dimension_semantics=("arbitrary",), vmem_limit_bytes=VMEM_LIMIT_BYTES),
        name="adaln_modulation",
    )(c.T, w_ada, b_ada.reshape(1, n_cols), *cast)
    return out[0], [arr.reshape(shape) for arr, shape in zip(out[1:], cast_shapes)]


def _norm_modulate(x, g, shift, scale):
    ms = jnp.mean(x * x, axis=-1, keepdims=True)
    col_gain = g * (1.0 + scale)
    return (x * lax.rsqrt(ms + EPS) * col_gain + shift).astype(jnp.bfloat16)


def _ffn_kernel(x_ref, mod_ref, g_ref, win_ref, wout_ref, *rest, mod_row, n_cast):
    cast_src, o_ref, cast_dst = rest[:n_cast], rest[n_cast], rest[n_cast + 1:]
    x = x_ref[...]
    shift = mod_ref[0, mod_row:mod_row + 1, :]
    scale = mod_ref[0, mod_row + 1:mod_row + 2, :]
    gate = mod_ref[0, mod_row + 2:mod_row + 3, :]
    h = _norm_modulate(x, g_ref[...], shift, scale)
    acc = None
    for c in range(N_FF_CHUNKS):
        lo = c * FF_CHUNK
        a = jnp.dot(h, win_ref[:, lo:lo + FF_CHUNK], preferred_element_type=jnp.float32)
        b = jnp.dot(h, win_ref[:, D_FF + lo:D_FF + lo + FF_CHUNK], preferred_element_type=jnp.float32)
        act = (a * jax.nn.sigmoid(a) * b).astype(jnp.bfloat16)
        part = jnp.dot(act, wout_ref[lo:lo + FF_CHUNK, :], preferred_element_type=jnp.float32)
        acc = part if acc is None else acc + part
    o_ref[...] = x + (0.5 * gate) * acc
    for src, dst in zip(cast_src, cast_dst):
        dst[...] = src[...].astype(dst.dtype)


def _ffn(x2d, mod, g, w_in, w_out, *, mod_row, tiles_per_batch, cast=()):
    n_rows = x2d.shape[0]
    n_steps = n_rows // FFN_ROW_TILE
    cast_shapes = [arr.shape for arr in cast]
    cast, cast_specs = _cast_specs(cast, n_steps)
    out = pl.pallas_call(
        functools.partial(_ffn_kernel, mod_row=mod_row, n_cast=len(cast)),
        grid=(n_steps,),
        in_specs=[
            pl.BlockSpec((FFN_ROW_TILE, D_MODEL), lambda i: (i, 0)),
            pl.BlockSpec((1, N_ADA, D_MODEL), lambda i: (i // tiles_per_batch, 0, 0)),
            _resident((1, D_MODEL)),
            _resident((D_MODEL, 2 * D_FF)),
            _resident((D_FF, D_MODEL)),
            *cast_specs,
        ],
        out_specs=[pl.BlockSpec((FFN_ROW_TILE, D_MODEL), lambda i: (i, 0)), *cast_specs],
        out_shape=[jax.ShapeDtypeStruct(x2d.shape, x2d.dtype),
                   *[jax.ShapeDtypeStruct(arr.shape, jnp.bfloat16) for arr in cast]],
        compiler_params=pltpu.CompilerParams(
            dimension_semantics=("arbitrary",), vmem_limit_bytes=VMEM_LIMIT_BYTES),
        name="swiglu_sublayer",
    )(x2d, mod, g.reshape(1, D_MODEL), w_in, w_out, *cast)
    return out[0], [arr.reshape(shape) for arr, shape in zip(out[1:], cast_shapes)]


def _head_rms_norm(t, gain):
    lane = lax.broadcasted_iota(jnp.int32, (t.shape[0], LANES), 1)
    low = lane < HEAD_DIM
    parts = []
    for p in range(HEAD_PAIRS):
        blk = t[:, p * LANES:(p + 1) * LANES]
        sq = blk * blk
        ms_low = jnp.sum(jnp.where(low, sq, 0.0), axis=-1, keepdims=True)
        ms_high = jnp.sum(jnp.where(low, 0.0, sq), axis=-1, keepdims=True)
        ms = jnp.where(low, ms_low, ms_high) * (1.0 / HEAD_DIM)
        parts.append(blk * lax.rsqrt(ms + EPS))
    return jnp.concatenate(parts, axis=-1) * gain


def _build_bias_table(diag_ref, bias_buf):
    r = lax.broadcasted_iota(jnp.int32, (Q_BLOCK, BIAS_PERIOD), 0)
    j = lax.broadcasted_iota(jnp.int32, (Q_BLOCK, BIAS_PERIOD), 1)
    rel_chunk = r // CHUNK + LEFT_CHUNKS - j // CHUNK
    in_band = jnp.logical_and(rel_chunk >= 0, rel_chunk <= LEFT_CHUNKS)
    for head in range(N_HEADS):
        diag = jnp.broadcast_to(diag_ref[head:head + 1, :], (Q_BLOCK, BIAS_PERIOD))
        table = pltpu.roll(diag, 0, 1, stride=1, stride_axis=0)
        table = jnp.where(in_band, table * LOG2_E, MASK_VALUE)
        row0 = (head % 2) * Q_BLOCK
        bias_buf[head // 2, row0:row0 + Q_BLOCK, :] = table[:, :KEY_WINDOW]
    bias_buf[HEAD_PAIRS:2 * HEAD_PAIRS] = jnp.full((HEAD_PAIRS, 2 * Q_BLOCK, KEY_WINDOW), MASK_VALUE,
                                                   bias_buf.dtype)


def _mix_kernel(x_ref, mod_ref, g_ref, win_ref, qg_ref, kg_ref, diag_ref, wa_ref, wg_ref,
                ps_ref, wp_ref, wo_ref, o_ref, k_buf, v_buf, u_buf, a_buf, bias_buf, wpool_buf):
    tile = pl.program_id(1)
    first = tile == 0

    @pl.when(jnp.logical_and(pl.program_id(0) == 0, first))
    def _():
        _build_bias_table(diag_ref, bias_buf)
        for gi in range(POOL_GROUPS):
            rows = slice(gi * POOL_GROUP_DIM, (gi + 1) * POOL_GROUP_DIM)
            scaled = (wg_ref[gi].astype(jnp.float32) * ps_ref[:, rows]).astype(jnp.bfloat16)
            wpool_buf[rows, :] = jnp.dot(scaled, wp_ref[rows, :],
                                         preferred_element_type=jnp.float32).astype(wpool_buf.dtype)
        for p in range(HEAD_PAIRS):
            v_buf[ROW_TILE:2 * ROW_TILE, p * V_GROUP + LANES:(p + 1) * V_GROUP] = jnp.ones(
                (ROW_TILE, LANES), v_buf.dtype)

    @pl.when(first)
    def _():
        k_buf[0:ROW_TILE, :] = jnp.zeros((ROW_TILE, ATTN_WIDTH), k_buf.dtype)
        v_buf[0:ROW_TILE, :] = jnp.zeros((ROW_TILE, HEAD_PAIRS * V_GROUP), v_buf.dtype)
        u_buf[0:POOL_HISTORY, :] = jnp.zeros((POOL_HISTORY, POOL_WIDTH), u_buf.dtype)

    x = x_ref[...]
    shift = mod_ref[0, 3:4, :]
    scale = mod_ref[0, 4:5, :]
    gate = mod_ref[0, 5:6, :]
    h = _norm_modulate(x, g_ref[...], shift, scale)

    def proj(col0, width):
        return jnp.dot(h, win_ref[:, col0:col0 + width], preferred_element_type=jnp.float32)

    q = proj(0, ATTN_WIDTH)
    k = proj(ATTN_WIDTH, ATTN_WIDTH)
    qn = (_head_rms_norm(q, qg_ref[...]) * (HEAD_DIM ** -0.5 * LOG2_E)).astype(jnp.bfloat16)
    k_buf[ROW_TILE:2 * ROW_TILE, :] = _head_rms_norm(k, kg_ref[...]).astype(jnp.bfloat16)
    v = proj(2 * ATTN_WIDTH, ATTN_WIDTH)
    u = proj(3 * ATTN_WIDTH, POOL_WIDTH)
    for p in range(HEAD_PAIRS):
        v_buf[ROW_TILE:2 * ROW_TILE, p * V_GROUP:p * V_GROUP + LANES] = (
            v[:, p * LANES:(p + 1) * LANES].astype(jnp.bfloat16))

    lane = lax.broadcasted_iota(jnp.int32, (Q_BLOCK, LANES), 1)
    low = lane < HEAD_DIM
    masked_table = first.astype(jnp.int32) * HEAD_PAIRS

    def scores(half, p):
        base = half * QK_ROWS
        cols = slice(p * LANES, (p + 1) * LANES)
        stack = []
        for blk in range(QK_ROWS // Q_BLOCK):
            q_pair = qn[base + blk * Q_BLOCK:base + (blk + 1) * Q_BLOCK, cols]
            zero = jnp.zeros_like(q_pair)
            stack += [jnp.where(low, q_pair, zero), jnp.where(low, zero, q_pair)]
        return lax.dot_general(jnp.concatenate(stack, axis=0), k_buf[base:base + QK_SPAN, cols],
                               (((1,), (1,)), ((), ())),
                               preferred_element_type=jnp.float32)

    def attend(half, p, s_all):
        cols = slice(p * LANES, (p + 1) * LANES)
        for blk in range(QK_ROWS // Q_BLOCK):
            row0 = half * QK_ROWS + blk * Q_BLOCK
            bias = jnp.concatenate(
                [bias_buf[(masked_table if c0 < ROW_TILE - row0 else 0) + p, :, c0:c0 + LANES]
                 for c0 in range(0, KEY_WINDOW, LANES)], axis=1)
            s = s_all[2 * blk * Q_BLOCK:2 * (blk + 1) * Q_BLOCK,
                      blk * Q_BLOCK:blk * Q_BLOCK + KEY_WINDOW] + bias
            m = jnp.max(s, axis=-1, keepdims=True)
            pexp = jnp.exp2((s - m).astype(jnp.bfloat16))
            o = jnp.dot(pexp, v_buf[row0:row0 + KEY_WINDOW, p * V_GROUP:(p + 1) * V_GROUP],
                        preferred_element_type=jnp.float32)
            o = o[:, :LANES] / o[:, LANES:]
            a_buf[row0:row0 + Q_BLOCK, cols] = jnp.where(
                low, o[:Q_BLOCK], o[Q_BLOCK:]).astype(a_buf.dtype)

    units = [(half, p) for half in range(ROW_TILE // QK_ROWS) for p in range(HEAD_PAIRS)]
    gate_col0 = 3 * ATTN_WIDTH + POOL_WIDTH
    gate_width = 2 * D_MODEL // len(units)
    gate_parts = [proj(gate_col0, gate_width)]
    s_next = scores(*units[0])
    for n, unit in enumerate(units):
        s_cur = s_next
        if n + 1 < len(units):
            s_next = scores(*units[n + 1])
            gate_parts.append(proj(gate_col0 + (n + 1) * gate_width, gate_width))
        attend(*unit, s_cur)
    gates = jnp.concatenate(gate_parts, axis=1)
    ya = jnp.dot(a_buf[...], wa_ref[...], preferred_element_type=jnp.float32)
    k_buf[0:ROW_TILE, :] = k_buf[ROW_TILE:2 * ROW_TILE, :]
    v_buf[0:ROW_TILE, :] = v_buf[ROW_TILE:2 * ROW_TILE, :]

    u_buf[POOL_HISTORY:POOL_HISTORY + ROW_TILE, :] = u
    pos = tile * ROW_TILE + lax.broadcasted_iota(jnp.int32, (ROW_TILE, 1), 0)
    pooled = []
    for gi, window in enumerate(POOL_WINDOWS):
        cols = slice(gi * POOL_GROUP_DIM, (gi + 1) * POOL_GROUP_DIM)
        total = None
        for j in range(window):
            term = u_buf[POOL_HISTORY - j:POOL_HISTORY - j + ROW_TILE, cols]
            total = term if total is None else total + term
        count = jnp.minimum(pos + 1, window).astype(jnp.float32)
        mixed = (total / count - u[:, cols]).astype(jnp.bfloat16)
        pooled.append(mixed)
    yb = jnp.dot(jnp.concatenate(pooled, axis=-1), wpool_buf[...], preferred_element_type=jnp.float32)
    u_buf[0:POOL_HISTORY, :] = u_buf[ROW_TILE:ROW_TILE + POOL_HISTORY, :]

    merged = (jax.nn.sigmoid(gates[:, :D_MODEL]) * ya
              + jax.nn.sigmoid(gates[:, D_MODEL:]) * yb).astype(jnp.bfloat16)
    o_ref[...] = x + gate * jnp.dot(merged, wo_ref[...], preferred_element_type=jnp.float32)


def _bias_diagonals(rel_bias):
    n_far = BAND - REL_CLIP + 1
    n_near = KEY_WINDOW - n_far
    far = rel_bias[:, 2 * REL_CLIP:]
    near = rel_bias[:, 2 * REL_CLIP - n_near:2 * REL_CLIP][:, ::-1]
    return jnp.concatenate(
        [jnp.broadcast_to(far, (N_HEADS, n_far)), near,
         jnp.broadcast_to(far, (N_HEADS, BIAS_PERIOD - KEY_WINDOW))], axis=1).astype(jnp.float32)


def _token_mix(x3d, mod, g, w_in, q_gain, k_gain, rel_bias, w_attn_out, w_pool_group, pool_scale,
               w_pool_out, w_o):
    n_batch, seq, _ = x3d.shape
    bf16 = jnp.bfloat16
    in_cols = w_in.shape[1]
    return pl.pallas_call(
        _mix_kernel,
        grid=(n_batch, seq // ROW_TILE),
        in_specs=[
            pl.BlockSpec((None, ROW_TILE, D_MODEL), lambda b, i: (b, i, 0)),
            pl.BlockSpec((1, N_ADA, D_MODEL), lambda b, i: (b, 0, 0)),
            _resident((1, D_MODEL)),
            _resident((D_MODEL, in_cols)),
            _resident((1, ATTN_WIDTH)),
            _resident((1, ATTN_WIDTH)),
            _resident((N_HEADS, BIAS_PERIOD)),
            _resident((ATTN_WIDTH, D_MODEL)),
            _resident((POOL_GROUPS, POOL_GROUP_DIM, POOL_GROUP_DIM)),
            _resident((1, POOL_WIDTH)),
            _resident((POOL_WIDTH, D_MODEL)),
            _resident((D_MODEL, D_MODEL)),
        ],
        out_specs=pl.BlockSpec((None, ROW_TILE, D_MODEL), lambda b, i: (b, i, 0)),
        out_shape=jax.ShapeDtypeStruct(x3d.shape, x3d.dtype),
        scratch_shapes=[
            pltpu.VMEM((2 * ROW_TILE, ATTN_WIDTH), bf16),
            pltpu.VMEM((2 * ROW_TILE, HEAD_PAIRS * V_GROUP), bf16),
            pltpu.VMEM((POOL_HISTORY + ROW_TILE, POOL_WIDTH), jnp.float32),
            pltpu.VMEM((ROW_TILE, ATTN_WIDTH), bf16),
            pltpu.VMEM((2 * HEAD_PAIRS, 2 * Q_BLOCK, KEY_WINDOW), jnp.float32),
            pltpu.VMEM((POOL_WIDTH, D_MODEL), bf16),
        ],
        compiler_params=pltpu.CompilerParams(
            dimension_semantics=("arbitrary", "arbitrary"), vmem_limit_bytes=VMEM_LIMIT_BYTES),
        name="token_mix_sublayer",
    )(x3d, mod, g.reshape(1, D_MODEL), w_in,
      jnp.tile(q_gain, N_HEADS).reshape(1, ATTN_WIDTH), jnp.tile(k_gain, N_HEADS).reshape(1, ATTN_WIDTH),
      _bias_diagonals(rel_bias), w_attn_out, w_pool_group, pool_scale.reshape(1, POOL_WIDTH), w_pool_out, w_o)


def kernel(x, c, w_ada, b_ada, g_ffn1, w_ffn1_in, w_ffn1_out, g_mix, w_in, q_gain, k_gain, rel_bias,
           w_attn_out, w_pool_group, pool_scale, w_pool_out, w_o, g_ffn2, w_ffn2_in, w_ffn2_out):
    n_batch, seq, d_model = x.shape
    depth = w_ada.shape[0]
    assert d_model == D_MODEL and seq % ROW_TILE == 0 and seq % FFN_ROW_TILE == 0
    tiles_per_batch = seq // FFN_ROW_TILE
    for l in range(depth):
        mod, (w_ffn1_in_b, w_ffn1_out_b) = _modulation(c, w_ada[l], b_ada[l],
                                                        cast=(w_ffn1_in[l], w_ffn1_out[l]))
        mod = mod.reshape(n_batch, N_ADA, D_MODEL)
        x2d = x.reshape(n_batch * seq, D_MODEL)
        later = (w_in[l], w_attn_out[l], w_pool_group[l].reshape(POOL_WIDTH, POOL_GROUP_DIM),
                 w_pool_out[l], w_o[l], w_ffn2_in[l], w_ffn2_out[l])
        x2d, later = _ffn(x2d, mod, g_ffn1[l], w_ffn1_in_b, w_ffn1_out_b, mod_row=0,
                          tiles_per_batch=tiles_per_batch, cast=later)
        w_in_b, w_attn_out_b, w_pool_group_b, w_pool_out_b, w_o_b, w_ffn2_in_b, w_ffn2_out_b = later
        x = _token_mix(x2d.reshape(n_batch, seq, D_MODEL), mod, g_mix[l], w_in_b, q_gain[l], k_gain[l],
                       rel_bias[l], w_attn_out_b,
                       w_pool_group_b.reshape(POOL_GROUPS, POOL_GROUP_DIM, POOL_GROUP_DIM),
                       pool_scale[l], w_pool_out_b, w_o_b)
        x2d, _ = _ffn(x.reshape(n_batch * seq, D_MODEL), mod, g_ffn2[l], w_ffn2_in_b, w_ffn2_out_b,
                      mod_row=6, tiles_per_batch=tiles_per_batch)
        x = x2d.reshape(n_batch, seq, D_MODEL)
    return x
```

```python
import functools
import math

import jax
import jax.numpy as jnp
from jax import lax
from jax.experimental import pallas as pl
from jax.experimental.pallas import tpu as pltpu

D_MODEL = 1024
CHUNK = 64
LEFT_CHUNKS = 8
N_HEADS = 8
HEAD_DIM = 64
ATTN_WIDTH = N_HEADS * HEAD_DIM
POOL_WINDOWS = (2, 4, 8, 16)
POOL_GROUPS = len(POOL_WINDOWS)
POOL_WIDTH = 512
POOL_GROUP_DIM = POOL_WIDTH // POOL_GROUPS
REL_CLIP = 128
D_FF = 2816
N_ADA = 9
EPS = 1e-6
MASK_VALUE = -1e30
LOG2_E = math.log2(math.e)

LANES = 128
MXU_DIM = 256
BF16_SUBLANES = 16
VMEM_LIMIT_BYTES = 56 * 1024 * 1024

ROW_TILE = 512
FFN_ROW_TILE = 1024
FF_CHUNK = MXU_DIM
N_FF_CHUNKS = D_FF // FF_CHUNK
BAND = LEFT_CHUNKS * CHUNK
Q_BLOCK = 2 * CHUNK
KEY_WINDOW = Q_BLOCK + BAND
QK_ROWS = 2 * Q_BLOCK
QK_SPAN = QK_ROWS + BAND
HEAD_PAIRS = ATTN_WIDTH // LANES
V_GROUP = 2 * LANES
BIAS_PERIOD = 768
POOL_HISTORY = 16
MOD_STEPS = 8

assert D_FF % FF_CHUNK == 0
assert ROW_TILE == BAND
assert ROW_TILE % QK_ROWS == 0 and 2 * HEAD_DIM == LANES
assert BIAS_PERIOD >= Q_BLOCK + KEY_WINDOW - 1 and BIAS_PERIOD % LANES == 0
assert POOL_HISTORY >= max(POOL_WINDOWS) - 1 and POOL_HISTORY % 8 == 0
assert D_MODEL % (MOD_STEPS * 8) == 0


def _resident(shape):
    zeros = (0,) * len(shape)
    return pl.BlockSpec(shape, lambda *_: zeros, pipeline_mode=pl.Buffered(1))


def _row_blockable(arr, n_blocks):
    rows, cols = arr.shape
    split = 1
    while (rows * split) % (n_blocks * BF16_SUBLANES) or (cols // split) % LANES:
        split *= 2
        assert cols % split == 0, (arr.shape, n_blocks)
    return arr.reshape(rows * split, cols // split)


def _cast_specs(cast, n_blocks):
    views = [_row_blockable(arr, n_blocks) for arr in cast]
    specs = [pl.BlockSpec((v.shape[0] // n_blocks, v.shape[1]), lambda i: (i, 0)) for v in views]
    return views, specs


def _mod_kernel(ct_ref, w_ref, b_ref, *rest, n_cast):
    cast_src, o_ref, cast_dst = rest[:n_cast], rest[n_cast], rest[n_cast + 1:]

    @pl.when(pl.program_id(0) == 0)
    def _():
        o_ref[...] = jnp.broadcast_to(b_ref[...], o_ref.shape)

    for b in range(ct_ref.shape[1]):
        col = ct_ref[:, b:b + 1]
        col = col * jax.nn.sigmoid(col)
        o_ref[b:b + 1, :] += jnp.sum(w_ref[...] * col, axis=0, keepdims=True)
    for src, dst in zip(cast_src, cast_dst):
        dst[...] = src[...].astype(dst.dtype)


def _modulation(c, w_ada, b_ada, cast=()):
    n_batch = c.shape[0]
    n_cols = w_ada.shape[1]
    rows = D_MODEL // MOD_STEPS
    cast_shapes = [arr.shape for arr in cast]
    cast, cast_specs = _cast_specs(cast, MOD_STEPS)
    out = pl.pallas_call(
        functools.partial(_mod_kernel, n_cast=len(cast)),
        grid=(MOD_STEPS,),
        in_specs=[
            pl.BlockSpec((rows, n_batch), lambda j: (j, 0)),
            pl.BlockSpec((rows, n_cols), lambda j: (j, 0)),
            pl.BlockSpec((1, n_cols), lambda j: (0, 0)),
            *cast_specs,
        ],
        out_specs=[pl.BlockSpec((n_batch, n_cols), lambda j: (0, 0)), *cast_specs],
        out_shape=[jax.ShapeDtypeStruct((n_batch, n_cols), jnp.float32),
                   *[jax.ShapeDtypeStruct(arr.shape, jnp.bfloat16) for arr in cast]],
        compiler_params=pltpu.CompilerParams(
            dimension_semantics=("arbitrary",), vmem_limit_bytes=VMEM_LIMIT_BYTES),
        name="adaln_modulation",
    )(c.T, w_ada, b_ada.reshape(1, n_cols), *cast)
    return out[0], [arr.reshape(shape) for arr, shape in zip(out[1:], cast_shapes)]


def _norm_modulate(x, g, shift, scale):
    ms = jnp.mean(x * x, axis=-1, keepdims=True)
    col_gain = g * (1.0 + scale)
    return (x * lax.rsqrt(ms + EPS) * col_gain + shift).astype(jnp.bfloat16)


def _ffn_kernel(x_ref, mod_ref, g_ref, win_ref, wout_ref, *rest, mod_row, n_cast):
    cast_src, o_ref, cast_dst = rest[:n_cast], rest[n_cast], rest[n_cast + 1:]
    x = x_ref[...]
    shift = mod_ref[0, mod_row:mod_row + 1, :]
    scale = mod_ref[0, mod_row + 1:mod_row + 2, :]
    gate = mod_ref[0, mod_row + 2:mod_row + 3, :]
    h = _norm_modulate(x, g_ref[...], shift, scale)
    acc = None
    for c in range(N_FF_CHUNKS):
        lo = c * FF_CHUNK
        a = jnp.dot(h, win_ref[:, lo:lo + FF_CHUNK], preferred_element_type=jnp.float32)
        b = jnp.dot(h, win_ref[:, D_FF + lo:D_FF + lo + FF_CHUNK], preferred_element_type=jnp.float32)
        act = (a * jax.nn.sigmoid(a) * b).astype(jnp.bfloat16)
        part = jnp.dot(act, wout_ref[lo:lo + FF_CHUNK, :], preferred_element_type=jnp.float32)
        acc = part if acc is None else acc + part
    o_ref[...] = x + (0.5 * gate) * acc
    for src, dst in zip(cast_src, cast_dst):
        dst[...] = src[...].astype(dst.dtype)


def _ffn(x2d, mod, g, w_in, w_out, *, mod_row, tiles_per_batch, cast=()):
    n_rows = x2d.shape[0]
    n_steps = n_rows // FFN_ROW_TILE
    cast_shapes = [arr.shape for arr in cast]
    cast, cast_specs = _cast_specs(cast, n_steps)
    out = pl.pallas_call(
        functools.partial(_ffn_kernel, mod_row=mod_row, n_cast=len(cast)),
        grid=(n_steps,),
        in_specs=[
            pl.BlockSpec((FFN_ROW_TILE, D_MODEL), lambda i: (i, 0)),
            pl.BlockSpec((1, N_ADA, D_MODEL), lambda i: (i // tiles_per_batch, 0, 0)),
            _resident((1, D_MODEL)),
            _resident((D_MODEL, 2 * D_FF)),
            _resident((D_FF, D_MODEL)),
            *cast_specs,
        ],
        out_specs=[pl.BlockSpec((FFN_ROW_TILE, D_MODEL), lambda i: (i, 0)), *cast_specs],
        out_shape=[jax.ShapeDtypeStruct(x2d.shape, x2d.dtype),
                   *[jax.ShapeDtypeStruct(arr.shape, jnp.bfloat16) for arr in cast]],
        compiler_params=pltpu.CompilerParams(
            dimension_semantics=("arbitrary",), vmem_limit_bytes=VMEM_LIMIT_BYTES),
        name="swiglu_sublayer",
    )(x2d, mod, g.reshape(1, D_MODEL), w_in, w_out, *cast)
    return out[0], [arr.reshape(shape) for arr, shape in zip(out[1:], cast_shapes)]


def _head_rms_norm(t, gain):
    lane = lax.broadcasted_iota(jnp.int32, (t.shape[0], LANES), 1)
    low = lane < HEAD_DIM
    parts = []
    for p in range(HEAD_PAIRS):
        blk = t[:, p * LANES:(p + 1) * LANES]
        sq = blk * blk
        ms_low = jnp.sum(jnp.where(low, sq, 0.0), axis=-1, keepdims=True)
        ms_high = jnp.sum(jnp.where(low, 0.0, sq), axis=-1, keepdims=True)
        ms = jnp.where(low, ms_low, ms_high) * (1.0 / HEAD_DIM)
        parts.append(blk * lax.rsqrt(ms + EPS))
    return jnp.concatenate(parts, axis=-1) * gain


def _build_bias_table(diag_ref, bias_buf):
    r = lax.broadcasted_iota(jnp.int32, (Q_BLOCK, BIAS_PERIOD), 0)
    j = lax.broadcasted_iota(jnp.int32, (Q_BLOCK, BIAS_PERIOD), 1)
    rel_chunk = r // CHUNK + LEFT_CHUNKS - j // CHUNK
    in_band = jnp.logical_and(rel_chunk >= 0, rel_chunk <= LEFT_CHUNKS)
    for head in range(N_HEADS):
        diag = jnp.broadcast_to(diag_ref[head:head + 1, :], (Q_BLOCK, BIAS_PERIOD))
        table = pltpu.roll(diag, 0, 1, stride=1, stride_axis=0)
        table = jnp.where(in_band, table * LOG2_E, MASK_VALUE)
        row0 = (head % 2) * Q_BLOCK
        bias_buf[head // 2, row0:row0 + Q_BLOCK, :] = table[:, :KEY_WINDOW]
    bias_buf[HEAD_PAIRS:2 * HEAD_PAIRS] = jnp.full((HEAD_PAIRS, 2 * Q_BLOCK, KEY_WINDOW), MASK_VALUE,
                                                   bias_buf.dtype)


def _mix_kernel(x_ref, mod_ref, g_ref, win_ref, qg_ref, kg_ref, diag_ref, wa_ref, wg_ref,
                ps_ref, wp_ref, wo_ref, o_ref, k_buf, v_buf, u_buf, a_buf, bias_buf, wpool_buf):
    tile = pl.program_id(1)
    first = tile == 0

    @pl.when(jnp.logical_and(pl.program_id(0) == 0, first))
    def _():
        _build_bias_table(diag_ref, bias_buf)
        for gi in range(POOL_GROUPS):
            rows = slice(gi * POOL_GROUP_DIM, (gi + 1) * POOL_GROUP_DIM)
            scaled = (wg_ref[gi].astype(jnp.float32) * ps_ref[:, rows]).astype(jnp.bfloat16)
            wpool_buf[rows, :] = jnp.dot(scaled, wp_ref[rows, :],
                                         preferred_element_type=jnp.float32).astype(wpool_buf.dtype)
        for p in range(HEAD_PAIRS):
            v_buf[ROW_TILE:2 * ROW_TILE, p * V_GROUP + LANES:(p + 1) * V_GROUP] = jnp.ones(
                (ROW_TILE, LANES), v_buf.dtype)

    @pl.when(first)
    def _():
        k_buf[0:ROW_TILE, :] = jnp.zeros((ROW_TILE, ATTN_WIDTH), k_buf.dtype)
        v_buf[0:ROW_TILE, :] = jnp.zeros((ROW_TILE, HEAD_PAIRS * V_GROUP), v_buf.dtype)
        u_buf[0:POOL_HISTORY, :] = jnp.zeros((POOL_HISTORY, POOL_WIDTH), u_buf.dtype)

    x = x_ref[...]
    shift = mod_ref[0, 3:4, :]
    scale = mod_ref[0, 4:5, :]
    gate = mod_ref[0, 5:6, :]
    h = _norm_modulate(x, g_ref[...], shift, scale)

    def proj(col0, width):
        return jnp.dot(h, win_ref[:, col0:col0 + width], preferred_element_type=jnp.float32)

    q = proj(0, ATTN_WIDTH)
    k = proj(ATTN_WIDTH, ATTN_WIDTH)
    qn = (_head_rms_norm(q, qg_ref[...]) * (HEAD_DIM ** -0.5 * LOG2_E)).astype(jnp.bfloat16)
    k_buf[ROW_TILE:2 * ROW_TILE, :] = _head_rms_norm(k, kg_ref[...]).astype(jnp.bfloat16)
    v = proj(2 * ATTN_WIDTH, ATTN_WIDTH)
    u = proj(3 * ATTN_WIDTH, POOL_WIDTH)
    for p in range(HEAD_PAIRS):
        v_buf[ROW_TILE:2 * ROW_TILE, p * V_GROUP:p * V_GROUP + LANES] = (
            v[:, p * LANES:(p + 1) * LANES].astype(jnp.bfloat16))

    lane = lax.broadcasted_iota(jnp.int32, (Q_BLOCK, LANES), 1)
    low = lane < HEAD_DIM
    masked_table = first.astype(jnp.int32) * HEAD_PAIRS

    def scores(half, p):
        base = half * QK_ROWS
        cols = slice(p * LANES, (p + 1) * LANES)
        stack = []
        for blk in range(QK_ROWS // Q_BLOCK):
            q_pair = qn[base + blk * Q_BLOCK:base + (blk + 1) * Q_BLOCK, cols]
            zero = jnp.zeros_like(q_pair)
            stack += [jnp.where(low, q_pair, zero), jnp.where(low, zero, q_pair)]
        return lax.dot_general(jnp.concatenate(stack, axis=0), k_buf[base:base + QK_SPAN, cols],
                               (((1,), (1,)), ((), ())),
                               preferred_element_type=jnp.float32)

    def attend(half, p, s_all):
        cols = slice(p * LANES, (p + 1) * LANES)
        for blk in range(QK_ROWS // Q_BLOCK):
            row0 = half * QK_ROWS + blk * Q_BLOCK
            bias = jnp.concatenate(
                [bias_buf[(masked_table if c0 < ROW_TILE - row0 else 0) + p, :, c0:c0 + LANES]
                 for c0 in range(0, KEY_WINDOW, LANES)], axis=1)
            s = s_all[2 * blk * Q_BLOCK:2 * (blk + 1) * Q_BLOCK,
                      blk * Q_BLOCK:blk * Q_BLOCK + KEY_WINDOW] + bias
            m = jnp.max(s, axis=-1, keepdims=True)
            pexp = jnp.exp2(s - m).astype(jnp.bfloat16)
            o = jnp.dot(pexp, v_buf[row0:row0 + KEY_WINDOW, p * V_GROUP:(p + 1) * V_GROUP],
                        preferred_element_type=jnp.float32)
            o = o[:, :LANES] / o[:, LANES:]
            a_buf[row0:row0 + Q_BLOCK, cols] = jnp.where(
                low, o[:Q_BLOCK], o[Q_BLOCK:]).astype(a_buf.dtype)

    units = [(half, p) for half in range(ROW_TILE // QK_ROWS) for p in range(HEAD_PAIRS)]
    gate_col0 = 3 * ATTN_WIDTH + POOL_WIDTH
    gate_width = 2 * D_MODEL // len(units)
    gate_parts = [proj(gate_col0, gate_width)]
    s_next = scores(*units[0])
    for n, unit in enumerate(units):
        s_cur = s_next
        if n + 1 < len(units):
            s_next = scores(*units[n + 1])
            gate_parts.append(proj(gate_col0 + (n + 1) * gate_width, gate_width))
        attend(*unit, s_cur)
    gates = jnp.concatenate(gate_parts, axis=1)
    ya = jnp.dot(a_buf[...], wa_ref[...], preferred_element_type=jnp.float32)
    k_buf[0:ROW_TILE, :] = k_buf[ROW_TILE:2 * ROW_TILE, :]
    v_buf[0:ROW_TILE, :] = v_buf[ROW_TILE:2 * ROW_TILE, :]

    u_buf[POOL_HISTORY:POOL_HISTORY + ROW_TILE, :] = u
    pos = tile * ROW_TILE + lax.broadcasted_iota(jnp.int32, (ROW_TILE, 1), 0)
    pooled = []
    for gi, window in enumerate(POOL_WINDOWS):
        cols = slice(gi * POOL_GROUP_DIM, (gi + 1) * POOL_GROUP_DIM)
        total = None
        for j in range(window):
            term = u_buf[POOL_HISTORY - j:POOL_HISTORY - j + ROW_TILE, cols]
            total = term if total is None else total + term
        count = jnp.minimum(pos + 1, window).astype(jnp.float32)
        mixed = (total / count - u[:, cols]).astype(jnp.bfloat16)
        pooled.append(mixed)
    yb = jnp.dot(jnp.concatenate(pooled, axis=-1), wpool_buf[...], preferred_element_type=jnp.float32)
    u_buf[0:POOL_HISTORY, :] = u_buf[ROW_TILE:ROW_TILE + POOL_HISTORY, :]

    merged = (jax.nn.sigmoid(gates[:, :D_MODEL]) * ya
              + jax.nn.sigmoid(gates[:, D_MODEL:]) * yb).astype(jnp.bfloat16)
    o_ref[...] = x + gate * jnp.dot(merged, wo_ref[...], preferred_element_type=jnp.float32)


def _bias_diagonals(rel_bias):
    n_far = BAND - REL_CLIP + 1
    n_near = KEY_WINDOW - n_far
    far = rel_bias[:, 2 * REL_CLIP:]
    near = rel_bias[:, 2 * REL_CLIP - n_near:2 * REL_CLIP][:, ::-1]
    return jnp.concatenate(
        [jnp.broadcast_to(far, (N_HEADS, n_far)), near,
         jnp.broadcast_to(far, (N_HEADS, BIAS_PERIOD - KEY_WINDOW))], axis=1).astype(jnp.float32)


def _token_mix(x3d, mod, g, w_in, q_gain, k_gain, rel_bias, w_attn_out, w_pool_group, pool_scale,
               w_pool_out, w_o):
    n_batch, seq, _ = x3d.shape
    bf16 = jnp.bfloat16
    in_cols = w_in.shape[1]
    return pl.pallas_call(
        _mix_kernel,
        grid=(n_batch, seq // ROW_TILE),
        in_specs=[
            pl.BlockSpec((None, ROW_TILE, D_MODEL), lambda b, i: (b, i, 0)),
            pl.BlockSpec((1, N_ADA, D_MODEL), lambda b, i: (b, 0, 0)),
            _resident((1, D_MODEL)),
            _resident((D_MODEL, in_cols)),
            _resident((1, ATTN_WIDTH)),
            _resident((1, ATTN_WIDTH)),
            _resident((N_HEADS, BIAS_PERIOD)),
            _resident((ATTN_WIDTH, D_MODEL)),
            _resident((POOL_GROUPS, POOL_GROUP_DIM, POOL_GROUP_DIM)),
            _resident((1, POOL_WIDTH)),
            _resident((POOL_WIDTH, D_MODEL)),
            _resident((D_MODEL, D_MODEL)),
        ],
        out_specs=pl.BlockSpec((None, ROW_TILE, D_MODEL), lambda b, i: (b, i, 0)),
        out_shape=jax.ShapeDtypeStruct(x3d.shape, x3d.dtype),
        scratch_shapes=[
            pltpu.VMEM((2 * ROW_TILE, ATTN_WIDTH), bf16),
            pltpu.VMEM((2 * ROW_TILE, HEAD_PAIRS * V_GROUP), bf16),
            pltpu.VMEM((POOL_HISTORY + ROW_TILE, POOL_WIDTH), jnp.float32),
            pltpu.VMEM((ROW_TILE, ATTN_WIDTH), bf16),
            pltpu.VMEM((2 * HEAD_PAIRS, 2 * Q_BLOCK, KEY_WINDOW), jnp.float32),
            pltpu.VMEM((POOL_WIDTH, D_MODEL), bf16),
        ],
        compiler_params=pltpu.CompilerParams(
            dimension_semantics=("arbitrary", "arbitrary"), vmem_limit_bytes=VMEM_LIMIT_BYTES),
        name="token_mix_sublayer",
    )(x3d, mod, g.reshape(1, D_MODEL), w_in,
      jnp.tile(q_gain, N_HEADS).reshape(1, ATTN_WIDTH), jnp.tile(k_gain, N_HEADS).reshape(1, ATTN_WIDTH),
      _bias_diagonals(rel_bias), w_attn_out, w_pool_group, pool_scale.reshape(1, POOL_WIDTH), w_pool_out, w_o)


def kernel(x, c, w_ada, b_ada, g_ffn1, w_ffn1_in, w_ffn1_out, g_mix, w_in, q_gain, k_gain, rel_bias,
           w_attn_out, w_pool_group, pool_scale, w_pool_out, w_o, g_ffn2, w_ffn2_in, w_ffn2_out):
    n_batch, seq, d_model = x.shape
    depth = w_ada.shape[0]
    assert d_model == D_MODEL and seq % ROW_TILE == 0 and seq % FFN_ROW_TILE == 0
    tiles_per_batch = seq // FFN_ROW_TILE
    for l in range(depth):
        mod, (w_ffn1_in_b, w_ffn1_out_b) = _modulation(c, w_ada[l], b_ada[l],
                                                        cast=(w_ffn1_in[l], w_ffn1_out[l]))
        mod = mod.reshape(n_batch, N_ADA, D_MODEL)
        x2d = x.reshape(n_batch * seq, D_MODEL)
        later = (w_in[l], w_attn_out[l], w_pool_group[l].reshape(POOL_WIDTH, POOL_GROUP_DIM),
                 w_pool_out[l], w_o[l], w_ffn2_in[l], w_ffn2_out[l])
        x2d, later = _ffn(x2d, mod, g_ffn1[l], w_ffn1_in_b, w_ffn1_out_b, mod_row=0,
                          tiles_per_batch=tiles_per_batch, cast=later)
        w_in_b, w_attn_out_b, w_pool_group_b, w_pool_out_b, w_o_b, w_ffn2_in_b, w_ffn2_out_b = later
        x = _token_mix(x2d.reshape(n_batch, seq, D_MODEL), mod, g_mix[l], w_in_b, q_gain[l], k_gain[l],
                       rel_bias[l], w_attn_out_b,
                       w_pool_group_b.reshape(POOL_GROUPS, POOL_GROUP_DIM, POOL_GROUP_DIM),
                       pool_scale[l], w_pool_out_b, w_o_b)
        x2d, _ = _ffn(x.reshape(n_batch * seq, D_MODEL), mod, g_ffn2[l], w_ffn2_in_b, w_ffn2_out_b,
                      mod_row=6, tiles_per_batch=tiles_per_batch)
        x = x2d.reshape(n_batch, seq, D_MODEL)
    return x
```

```python
import functools
import math

import jax
import jax.numpy as jnp
from jax import lax
from jax.experimental import pallas as pl
from jax.experimental.pallas import tpu as pltpu

D_MODEL = 1024
CHUNK = 64
LEFT_CHUNKS = 8
N_HEADS = 8
HEAD_DIM = 64
ATTN_WIDTH = N_HEADS * HEAD_DIM
POOL_WINDOWS = (2, 4, 8, 16)
POOL_GROUPS = len(POOL_WINDOWS)
POOL_WIDTH = 512
POOL_GROUP_DIM = POOL_WIDTH // POOL_GROUPS
REL_CLIP = 128
D_FF = 2816
N_ADA = 9
EPS = 1e-6
MASK_VALUE = -1e30
LOG2_E = math.log2(math.e)

LANES = 128
MXU_DIM = 256
BF16_SUBLANES = 16
VMEM_LIMIT_BYTES = 56 * 1024 * 1024

ROW_TILE = 512
FFN_ROW_TILE = 1024
FF_CHUNK = MXU_DIM
N_FF_CHUNKS = D_FF // FF_CHUNK
BAND = LEFT_CHUNKS * CHUNK
Q_BLOCK = 2 * CHUNK
KEY_WINDOW = Q_BLOCK + BAND
QK_ROWS = 2 * Q_BLOCK
QK_SPAN = QK_ROWS + BAND
HEAD_PAIRS = ATTN_WIDTH // LANES
V_GROUP = 2 * LANES
BIAS_PERIOD = 768
POOL_HISTORY = 16
MOD_STEPS = 8

assert D_FF % FF_CHUNK == 0
assert ROW_TILE == BAND
assert ROW_TILE % QK_ROWS == 0 and 2 * HEAD_DIM == LANES
assert BIAS_PERIOD >= Q_BLOCK + KEY_WINDOW - 1 and BIAS_PERIOD % LANES == 0
assert POOL_HISTORY >= max(POOL_WINDOWS) - 1 and POOL_HISTORY % 8 == 0
assert all(w & (w - 1) == 0 for w in POOL_WINDOWS) and list(POOL_WINDOWS) == sorted(POOL_WINDOWS)
assert D_MODEL % (MOD_STEPS * 8) == 0


def _resident(shape):
    zeros = (0,) * len(shape)
    return pl.BlockSpec(shape, lambda *_: zeros, pipeline_mode=pl.Buffered(1))


def _row_blockable(arr, n_blocks):
    rows, cols = arr.shape
    split = 1
    while (rows * split) % (n_blocks * BF16_SUBLANES) or (cols // split) % LANES:
        split *= 2
        assert cols % split == 0, (arr.shape, n_blocks)
    return arr.reshape(rows * split, cols // split)


def _cast_specs(cast, n_blocks):
    views = [_row_blockable(arr, n_blocks) for arr in cast]
    specs = [pl.BlockSpec((v.shape[0] // n_blocks, v.shape[1]), lambda i: (i, 0)) for v in views]
    return views, specs


def _mod_kernel(ct_ref, w_ref, b_ref, *rest, n_cast):
    cast_src, o_ref, cast_dst = rest[:n_cast], rest[n_cast], rest[n_cast + 1:]

    @pl.when(pl.program_id(0) == 0)
    def _():
        o_ref[...] = jnp.broadcast_to(b_ref[...], o_ref.shape)

    for b in range(ct_ref.shape[1]):
        col = ct_ref[:, b:b + 1]
        col = col * jax.nn.sigmoid(col)
        o_ref[b:b + 1, :] += jnp.sum(w_ref[...] * col, axis=0, keepdims=True)
    for src, dst in zip(cast_src, cast_dst):
        dst[...] = src[...].astype(dst.dtype)


def _modulation(c, w_ada, b_ada, cast=()):
    n_batch = c.shape[0]
    n_cols = w_ada.shape[1]
    rows = D_MODEL // MOD_STEPS
    cast_shapes = [arr.shape for arr in cast]
    cast, cast_specs = _cast_specs(cast, MOD_STEPS)
    out = pl.pallas_call(
        functools.partial(_mod_kernel, n_cast=len(cast)),
        grid=(MOD_STEPS,),
        in_specs=[
            pl.BlockSpec((rows, n_batch), lambda j: (j, 0)),
            pl.BlockSpec((rows, n_cols), lambda j: (j, 0)),
            pl.BlockSpec((1, n_cols), lambda j: (0, 0)),
            *cast_specs,
        ],
        out_specs=[pl.BlockSpec((n_batch, n_cols), lambda j: (0, 0)), *cast_specs],
        out_shape=[jax.ShapeDtypeStruct((n_batch, n_cols), jnp.float32),
                   *[jax.ShapeDtypeStruct(arr.shape, jnp.bfloat16) for arr in cast]],
        compiler_params=pltpu.CompilerParams(
            dimension_semantics=("arbitrary",), vmem_limit_bytes=VMEM_LIMIT_BYTES),
        name="adaln_modulation",
    )(c.T, w_ada, b_ada.reshape(1, n_cols), *cast)
    return out[0], [arr.reshape(shape) for arr, shape in zip(out[1:], cast_shapes)]


def _norm_modulate(x, g, shift, scale):
    ms = jnp.mean(x * x, axis=-1, keepdims=True)
    col_gain = g * (1.0 + scale)
    return (x * lax.rsqrt(ms + EPS) * col_gain + shift).astype(jnp.bfloat16)


def _ffn_kernel(x_ref, mod_ref, g_ref, win_ref, wout_ref, *rest, mod_row, n_cast):
    cast_src, o_ref, cast_dst = rest[:n_cast], rest[n_cast], rest[n_cast + 1:]
    x = x_ref[...]
    shift = mod_ref[0, mod_row:mod_row + 1, :]
    scale = mod_ref[0, mod_row + 1:mod_row + 2, :]
    gate = mod_ref[0, mod_row + 2:mod_row + 3, :]
    h = _norm_modulate(x, g_ref[...], shift, scale)
    acc = None
    for c in range(N_FF_CHUNKS):
        lo = c * FF_CHUNK
        a = jnp.dot(h, win_ref[:, lo:lo + FF_CHUNK], preferred_element_type=jnp.float32)
        b = jnp.dot(h, win_ref[:, D_FF + lo:D_FF + lo + FF_CHUNK], preferred_element_type=jnp.float32)
        act = (a * jax.nn.sigmoid(a) * b).astype(jnp.bfloat16)
        part = jnp.dot(act, wout_ref[lo:lo + FF_CHUNK, :], preferred_element_type=jnp.float32)
        acc = part if acc is None else acc + part
    o_ref[...] = x + (0.5 * gate) * acc
    for src, dst in zip(cast_src, cast_dst):
        dst[...] = src[...].astype(dst.dtype)


def _ffn(x2d, mod, g, w_in, w_out, *, mod_row, tiles_per_batch, cast=()):
    n_rows = x2d.shape[0]
    n_steps = n_rows // FFN_ROW_TILE
    cast_shapes = [arr.shape for arr in cast]
    cast, cast_specs = _cast_specs(cast, n_steps)
    out = pl.pallas_call(
        functools.partial(_ffn_kernel, mod_row=mod_row, n_cast=len(cast)),
        grid=(n_steps,),
        in_specs=[
            pl.BlockSpec((FFN_ROW_TILE, D_MODEL), lambda i: (i, 0)),
            pl.BlockSpec((1, N_ADA, D_MODEL), lambda i: (i // tiles_per_batch, 0, 0)),
            _resident((1, D_MODEL)),
            _resident((D_MODEL, 2 * D_FF)),
            _resident((D_FF, D_MODEL)),
            *cast_specs,
        ],
        out_specs=[pl.BlockSpec((FFN_ROW_TILE, D_MODEL), lambda i: (i, 0)), *cast_specs],
        out_shape=[jax.ShapeDtypeStruct(x2d.shape, x2d.dtype),
                   *[jax.ShapeDtypeStruct(arr.shape, jnp.bfloat16) for arr in cast]],
        compiler_params=pltpu.CompilerParams(
            dimension_semantics=("arbitrary",), vmem_limit_bytes=VMEM_LIMIT_BYTES),
        name="swiglu_sublayer",
    )(x2d, mod, g.reshape(1, D_MODEL), w_in, w_out, *cast)
    return out[0], [arr.reshape(shape) for arr, shape in zip(out[1:], cast_shapes)]


def _head_rms_norm(t, gain):
    lane = lax.broadcasted_iota(jnp.int32, (t.shape[0], LANES), 1)
    low = lane < HEAD_DIM
    parts = []
    for p in range(HEAD_PAIRS):
        blk = t[:, p * LANES:(p + 1) * LANES]
        sq = blk * blk
        r_low = lax.rsqrt(jnp.sum(jnp.where(low, sq, 0.0), axis=-1, keepdims=True) * (1.0 / HEAD_DIM) + EPS)
        r_high = lax.rsqrt(jnp.sum(jnp.where(low, 0.0, sq), axis=-1, keepdims=True) * (1.0 / HEAD_DIM) + EPS)
        scaled = blk * jnp.where(low, r_low, r_high) * gain[:, p * LANES:(p + 1) * LANES]
        parts.append(scaled.astype(jnp.bfloat16))
    return jnp.concatenate(parts, axis=-1)


def _build_bias_table(diag_ref, bias_buf):
    r = lax.broadcasted_iota(jnp.int32, (Q_BLOCK, BIAS_PERIOD), 0)
    j = lax.broadcasted_iota(jnp.int32, (Q_BLOCK, BIAS_PERIOD), 1)
    rel_chunk = r // CHUNK + LEFT_CHUNKS - j // CHUNK
    in_band = jnp.logical_and(rel_chunk >= 0, rel_chunk <= LEFT_CHUNKS)
    for head in range(N_HEADS):
        diag = jnp.broadcast_to(diag_ref[head:head + 1, :], (Q_BLOCK, BIAS_PERIOD))
        table = pltpu.roll(diag, 0, 1, stride=1, stride_axis=0)
        table = jnp.where(in_band, table * LOG2_E, MASK_VALUE)
        row0 = (head % 2) * Q_BLOCK
        bias_buf[head // 2, row0:row0 + Q_BLOCK, :] = table[:, :KEY_WINDOW]
    bias_buf[HEAD_PAIRS:2 * HEAD_PAIRS] = jnp.full((HEAD_PAIRS, 2 * Q_BLOCK, KEY_WINDOW), MASK_VALUE,
                                                   bias_buf.dtype)


def _mix_kernel(x_ref, mod_ref, g_ref, win_ref, qg_ref, kg_ref, diag_ref, wa_ref, wg_ref,
                ps_ref, wp_ref, wo_ref, o_ref, k_buf, v_buf, u_buf, a_buf, bias_buf, wpool_buf):
    tile = pl.program_id(1)
    first = tile == 0

    @pl.when(jnp.logical_and(pl.program_id(0) == 0, first))
    def _():
        _build_bias_table(diag_ref, bias_buf)
        for gi in range(POOL_GROUPS):
            rows = slice(gi * POOL_GROUP_DIM, (gi + 1) * POOL_GROUP_DIM)
            scaled = (wg_ref[gi].astype(jnp.float32) * ps_ref[:, rows]).astype(jnp.bfloat16)
            wpool_buf[rows, :] = jnp.dot(scaled, wp_ref[rows, :],
                                         preferred_element_type=jnp.float32).astype(wpool_buf.dtype)
        for p in range(HEAD_PAIRS):
            v_buf[ROW_TILE:2 * ROW_TILE, p * V_GROUP + LANES:(p + 1) * V_GROUP] = jnp.ones(
                (ROW_TILE, LANES), v_buf.dtype)

    @pl.when(first)
    def _():
        k_buf[0:ROW_TILE, :] = jnp.zeros((ROW_TILE, ATTN_WIDTH), k_buf.dtype)
        v_buf[0:ROW_TILE, :] = jnp.zeros((ROW_TILE, HEAD_PAIRS * V_GROUP), v_buf.dtype)
        u_buf[0:POOL_HISTORY, :] = jnp.zeros((POOL_HISTORY, POOL_WIDTH), u_buf.dtype)

    x = x_ref[...]
    shift = mod_ref[0, 3:4, :]
    scale = mod_ref[0, 4:5, :]
    gate = mod_ref[0, 5:6, :]
    h = _norm_modulate(x, g_ref[...], shift, scale)

    def proj(col0, width):
        return jnp.dot(h, win_ref[:, col0:col0 + width], preferred_element_type=jnp.float32)

    u = proj(3 * ATTN_WIDTH, POOL_WIDTH)
    u_buf[POOL_HISTORY:POOL_HISTORY + ROW_TILE, :] = u
    pos = tile * ROW_TILE + lax.broadcasted_iota(jnp.int32, (ROW_TILE, 1), 0)
    sums = u_buf[...]
    pooled = []
    span = 1
    for gi, window in enumerate(POOL_WINDOWS):
        while span < window:
            sums = sums + pltpu.roll(sums, span, axis=0)
            span *= 2
        total = sums[POOL_HISTORY:, :POOL_GROUP_DIM]
        if gi + 1 < POOL_GROUPS:
            sums = sums[:, POOL_GROUP_DIM:]
        count = jnp.minimum(pos + 1, window).astype(jnp.float32)
        u_group = u[:, gi * POOL_GROUP_DIM:(gi + 1) * POOL_GROUP_DIM]
        pooled.append((total / count - u_group).astype(jnp.bfloat16))
    pooled = jnp.concatenate(pooled, axis=-1)
    u_buf[0:POOL_HISTORY, :] = u_buf[ROW_TILE:ROW_TILE + POOL_HISTORY, :]

    q = proj(0, ATTN_WIDTH)
    qn = _head_rms_norm(q, qg_ref[...] * (HEAD_DIM ** -0.5 * LOG2_E))
    k = proj(ATTN_WIDTH, ATTN_WIDTH)
    k_buf[ROW_TILE:2 * ROW_TILE, :] = _head_rms_norm(k, kg_ref[...])
    v = proj(2 * ATTN_WIDTH, ATTN_WIDTH)
    for p in range(HEAD_PAIRS):
        v_buf[ROW_TILE:2 * ROW_TILE, p * V_GROUP:p * V_GROUP + LANES] = (
            v[:, p * LANES:(p + 1) * LANES].astype(jnp.bfloat16))

    lane = lax.broadcasted_iota(jnp.int32, (Q_BLOCK, LANES), 1)
    low = lane < HEAD_DIM
    masked_table = first.astype(jnp.int32) * HEAD_PAIRS

    def scores(half, p):
        base = half * QK_ROWS
        cols = slice(p * LANES, (p + 1) * LANES)
        stack = []
        for blk in range(QK_ROWS // Q_BLOCK):
            q_pair = qn[base + blk * Q_BLOCK:base + (blk + 1) * Q_BLOCK, cols]
            zero = jnp.zeros_like(q_pair)
            stack += [jnp.where(low, q_pair, zero), jnp.where(low, zero, q_pair)]
        return lax.dot_general(jnp.concatenate(stack, axis=0), k_buf[base:base + QK_SPAN, cols],
                               (((1,), (1,)), ((), ())),
                               preferred_element_type=jnp.float32)

    def attend(half, p, s_all):
        cols = slice(p * LANES, (p + 1) * LANES)
        for blk in range(QK_ROWS // Q_BLOCK):
            row0 = half * QK_ROWS + blk * Q_BLOCK
            bias = jnp.concatenate(
                [bias_buf[(masked_table if c0 < ROW_TILE - row0 else 0) + p, :, c0:c0 + LANES]
                 for c0 in range(0, KEY_WINDOW, LANES)], axis=1)
            s = s_all[2 * blk * Q_BLOCK:2 * (blk + 1) * Q_BLOCK,
                      blk * Q_BLOCK:blk * Q_BLOCK + KEY_WINDOW] + bias
            m = jnp.max(s, axis=-1, keepdims=True)
            pexp = jnp.exp2(s - m).astype(jnp.bfloat16)
            o = jnp.dot(pexp, v_buf[row0:row0 + KEY_WINDOW, p * V_GROUP:(p + 1) * V_GROUP],
                        preferred_element_type=jnp.float32)
            o = o[:, :LANES] / o[:, LANES:]
            a_buf[row0:row0 + Q_BLOCK, cols] = jnp.where(
                low, o[:Q_BLOCK], o[Q_BLOCK:]).astype(a_buf.dtype)

    units = [(half, p) for half in range(ROW_TILE // QK_ROWS) for p in range(HEAD_PAIRS)]
    gate_col0 = 3 * ATTN_WIDTH + POOL_WIDTH
    gate_width = 2 * D_MODEL // len(units)
    gate_parts = [proj(gate_col0, gate_width)]
    s_next = scores(*units[0])
    for n, unit in enumerate(units):
        s_cur = s_next
        if n + 1 < len(units):
            s_next = scores(*units[n + 1])
            gate_parts.append(proj(gate_col0 + (n + 1) * gate_width, gate_width))
        attend(*unit, s_cur)
    gates = jnp.concatenate(gate_parts, axis=1)
    ya = jnp.dot(a_buf[...], wa_ref[...], preferred_element_type=jnp.float32)
    k_buf[0:ROW_TILE, :] = k_buf[ROW_TILE:2 * ROW_TILE, :]
    v_buf[0:ROW_TILE, :] = v_buf[ROW_TILE:2 * ROW_TILE, :]

    yb = jnp.dot(pooled, wpool_buf[...], preferred_element_type=jnp.float32)

    merged = (jax.nn.sigmoid(gates[:, :D_MODEL]) * ya
              + jax.nn.sigmoid(gates[:, D_MODEL:]) * yb).astype(jnp.bfloat16)
    o_ref[...] = x + gate * jnp.dot(merged, wo_ref[...], preferred_element_type=jnp.float32)


def _bias_diagonals(rel_bias):
    n_far = BAND - REL_CLIP + 1
    n_near = KEY_WINDOW - n_far
    far = rel_bias[:, 2 * REL_CLIP:]
    near = rel_bias[:, 2 * REL_CLIP - n_near:2 * REL_CLIP][:, ::-1]
    return jnp.concatenate(
        [jnp.broadcast_to(far, (N_HEADS, n_far)), near,
         jnp.broadcast_to(far, (N_HEADS, BIAS_PERIOD - KEY_WINDOW))], axis=1).astype(jnp.float32)


def _token_mix(x3d, mod, g, w_in, q_gain, k_gain, rel_bias, w_attn_out, w_pool_group, pool_scale,
               w_pool_out, w_o):
    n_batch, seq, _ = x3d.shape
    bf16 = jnp.bfloat16
    in_cols = w_in.shape[1]
    return pl.pallas_call(
        _mix_kernel,
        grid=(n_batch, seq // ROW_TILE),
        in_specs=[
            pl.BlockSpec((None, ROW_TILE, D_MODEL), lambda b, i: (b, i, 0)),
            pl.BlockSpec((1, N_ADA, D_MODEL), lambda b, i: (b, 0, 0)),
            _resident((1, D_MODEL)),
            _resident((D_MODEL, in_cols)),
            _resident((1, ATTN_WIDTH)),
            _resident((1, ATTN_WIDTH)),
            _resident((N_HEADS, BIAS_PERIOD)),
            _resident((ATTN_WIDTH, D_MODEL)),
            _resident((POOL_GROUPS, POOL_GROUP_DIM, POOL_GROUP_DIM)),
            _resident((1, POOL_WIDTH)),
            _resident((POOL_WIDTH, D_MODEL)),
            _resident((D_MODEL, D_MODEL)),
        ],
        out_specs=pl.BlockSpec((None, ROW_TILE, D_MODEL), lambda b, i: (b, i, 0)),
        out_shape=jax.ShapeDtypeStruct(x3d.shape, x3d.dtype),
        scratch_shapes=[
            pltpu.VMEM((2 * ROW_TILE, ATTN_WIDTH), bf16),
            pltpu.VMEM((2 * ROW_TILE, HEAD_PAIRS * V_GROUP), bf16),
            pltpu.VMEM((POOL_HISTORY + ROW_TILE, POOL_WIDTH), jnp.float32),
            pltpu.VMEM((ROW_TILE, ATTN_WIDTH), bf16),
            pltpu.VMEM((2 * HEAD_PAIRS, 2 * Q_BLOCK, KEY_WINDOW), jnp.float32),
            pltpu.VMEM((POOL_WIDTH, D_MODEL), bf16),
        ],
        compiler_params=pltpu.CompilerParams(
            dimension_semantics=("arbitrary", "arbitrary"), vmem_limit_bytes=VMEM_LIMIT_BYTES),
        name="token_mix_sublayer",
    )(x3d, mod, g.reshape(1, D_MODEL), w_in,
      jnp.tile(q_gain, N_HEADS).reshape(1, ATTN_WIDTH), jnp.tile(k_gain, N_HEADS).reshape(1, ATTN_WIDTH),
      _bias_diagonals(rel_bias), w_attn_out, w_pool_group, pool_scale.reshape(1, POOL_WIDTH), w_pool_out, w_o)


def kernel(x, c, w_ada, b_ada, g_ffn1, w_ffn1_in, w_ffn1_out, g_mix, w_in, q_gain, k_gain, rel_bias,
           w_attn_out, w_pool_group, pool_scale, w_pool_out, w_o, g_ffn2, w_ffn2_in, w_ffn2_out):
    n_batch, seq, d_model = x.shape
    depth = w_ada.shape[0]
    assert d_model == D_MODEL and seq % ROW_TILE == 0 and seq % FFN_ROW_TILE == 0
    tiles_per_batch = seq // FFN_ROW_TILE
    for l in range(depth):
        mod, (w_ffn1_in_b, w_ffn1_out_b) = _modulation(c, w_ada[l], b_ada[l],
                                                        cast=(w_ffn1_in[l], w_ffn1_out[l]))
        mod = mod.reshape(n_batch, N_ADA, D_MODEL)
        x2d = x.reshape(n_batch * seq, D_MODEL)
        later = (w_in[l], w_attn_out[l], w_pool_group[l].reshape(POOL_WIDTH, POOL_GROUP_DIM),
                 w_pool_out[l], w_o[l], w_ffn2_in[l], w_ffn2_out[l])
        x2d, later = _ffn(x2d, mod, g_ffn1[l], w_ffn1_in_b, w_ffn1_out_b, mod_row=0,
                          tiles_per_batch=tiles_per_batch, cast=later)
        w_in_b, w_attn_out_b, w_pool_group_b, w_pool_out_b, w_o_b, w_ffn2_in_b, w_ffn2_out_b = later
        x = _token_mix(x2d.reshape(n_batch, seq, D_MODEL), mod, g_mix[l], w_in_b, q_gain[l], k_gain[l],
                       rel_bias[l], w_attn_out_b,
                       w_pool_group_b.reshape(POOL_GROUPS, POOL_GROUP_DIM, POOL_GROUP_DIM),
                       pool_scale[l], w_pool_out_b, w_o_b)
        x2d, _ = _ffn(x.reshape(n_batch * seq, D_MODEL), mod, g_ffn2[l], w_ffn2_in_b, w_ffn2_out_b,
                      mod_row=6, tiles_per_batch=tiles_per_batch)
        x = x2d.reshape(n_batch, seq, D_MODEL)
    return x
```

```python
import functools
import math

import jax
import jax.numpy as jnp
from jax import lax
from jax.experimental import pallas as pl
from jax.experimental.pallas import tpu as pltpu

D_MODEL = 1024
CHUNK = 64
LEFT_CHUNKS = 8
N_HEADS = 8
HEAD_DIM = 64
ATTN_WIDTH = N_HEADS * HEAD_DIM
POOL_WINDOWS = (2, 4, 8, 16)
POOL_GROUPS = len(POOL_WINDOWS)
POOL_WIDTH = 512
POOL_GROUP_DIM = POOL_WIDTH // POOL_GROUPS
REL_CLIP = 128
D_FF = 2816
N_ADA = 9
EPS = 1e-6
MASK_VALUE = -1e30
LOG2_E = math.log2(math.e)

LANES = 128
MXU_DIM = 256
BF16_SUBLANES = 16
VMEM_LIMIT_BYTES = 56 * 1024 * 1024

ROW_TILE = 512
FFN_ROW_TILE = 1024
FFN_NORM_ROWS = 256
FF_CHUNK = MXU_DIM
N_FF_CHUNKS = D_FF // FF_CHUNK
BAND = LEFT_CHUNKS * CHUNK
Q_BLOCK = 2 * CHUNK
KEY_WINDOW = Q_BLOCK + BAND
QK_ROWS = 2 * Q_BLOCK
QK_SPAN = QK_ROWS + BAND
HEAD_PAIRS = ATTN_WIDTH // LANES
V_GROUP = 2 * LANES
BIAS_PERIOD = 768
POOL_HISTORY = 16
MOD_STEPS = 8

assert D_FF % FF_CHUNK == 0
assert ROW_TILE == BAND
assert ROW_TILE % QK_ROWS == 0 and 2 * HEAD_DIM == LANES
assert BIAS_PERIOD >= Q_BLOCK + KEY_WINDOW - 1 and BIAS_PERIOD % LANES == 0
assert POOL_HISTORY >= max(POOL_WINDOWS) - 1 and POOL_HISTORY % 8 == 0
assert all(w & (w - 1) == 0 for w in POOL_WINDOWS) and list(POOL_WINDOWS) == sorted(POOL_WINDOWS)
assert D_MODEL % (MOD_STEPS * 8) == 0


def _resident(shape):
    zeros = (0,) * len(shape)
    return pl.BlockSpec(shape, lambda *_: zeros, pipeline_mode=pl.Buffered(1))


def _row_blockable(arr, n_blocks):
    rows, cols = arr.shape
    split = 1
    while (rows * split) % (n_blocks * BF16_SUBLANES) or (cols // split) % LANES:
        split *= 2
        assert cols % split == 0, (arr.shape, n_blocks)
    return arr.reshape(rows * split, cols // split)


def _cast_specs(cast, n_blocks):
    views = [_row_blockable(arr, n_blocks) for arr in cast]
    specs = [pl.BlockSpec((v.shape[0] // n_blocks, v.shape[1]), lambda i: (i, 0)) for v in views]
    return views, specs


def _mod_kernel(ct_ref, w_ref, b_ref, *rest, n_cast):
    cast_src, o_ref, cast_dst = rest[:n_cast], rest[n_cast], rest[n_cast + 1:]

    @pl.when(pl.program_id(0) == 0)
    def _():
        o_ref[...] = jnp.broadcast_to(b_ref[...], o_ref.shape)

    for b in range(ct_ref.shape[1]):
        col = ct_ref[:, b:b + 1]
        col = col * jax.nn.sigmoid(col)
        o_ref[b:b + 1, :] += jnp.sum(w_ref[...] * col, axis=0, keepdims=True)
    for src, dst in zip(cast_src, cast_dst):
        dst[...] = src[...].astype(dst.dtype)


def _modulation(c, w_ada, b_ada, cast=()):
    n_batch = c.shape[0]
    n_cols = w_ada.shape[1]
    rows = D_MODEL // MOD_STEPS
    cast_shapes = [arr.shape for arr in cast]
    cast, cast_specs = _cast_specs(cast, MOD_STEPS)
    out = pl.pallas_call(
        functools.partial(_mod_kernel, n_cast=len(cast)),
        grid=(MOD_STEPS,),
        in_specs=[
            pl.BlockSpec((rows, n_batch), lambda j: (j, 0)),
            pl.BlockSpec((rows, n_cols), lambda j: (j, 0)),
            pl.BlockSpec((1, n_cols), lambda j: (0, 0)),
            *cast_specs,
        ],
        out_specs=[pl.BlockSpec((n_batch, n_cols), lambda j: (0, 0)), *cast_specs],
        out_shape=[jax.ShapeDtypeStruct((n_batch, n_cols), jnp.float32),
                   *[jax.ShapeDtypeStruct(arr.shape, jnp.bfloat16) for arr in cast]],
        compiler_params=pltpu.CompilerParams(
            dimension_semantics=("arbitrary",), vmem_limit_bytes=VMEM_LIMIT_BYTES),
        name="adaln_modulation",
    )(c.T, w_ada, b_ada.reshape(1, n_cols), *cast)
    return out[0], [arr.reshape(shape) for arr, shape in zip(out[1:], cast_shapes)]


def _norm_modulate(x, g, shift, scale):
    ms = jnp.mean(x * x, axis=-1, keepdims=True)
    col_gain = g * (1.0 + scale)
    return (x * lax.rsqrt(ms + EPS) * col_gain + shift).astype(jnp.bfloat16)


def _ffn_kernel(x_ref, mod_ref, g_ref, win_ref, wout_ref, *rest, mod_row, n_cast):
    cast_src, o_ref, cast_dst = rest[:n_cast], rest[n_cast], rest[n_cast + 1:]
    x = x_ref[...]
    shift = mod_ref[0, mod_row:mod_row + 1, :]
    scale = mod_ref[0, mod_row + 1:mod_row + 2, :]
    gate = mod_ref[0, mod_row + 2:mod_row + 3, :]
    blocks = [_norm_modulate(x[r0:r0 + FFN_NORM_ROWS], g_ref[...], shift, scale)
              for r0 in range(0, FFN_ROW_TILE, FFN_NORM_ROWS)]
    h = jnp.concatenate(blocks, axis=0)
    acc = None
    for c in range(N_FF_CHUNKS):
        lo = c * FF_CHUNK
        w_a = win_ref[:, lo:lo + FF_CHUNK]
        w_b = win_ref[:, D_FF + lo:D_FF + lo + FF_CHUNK]
        if c == 0:
            a = jnp.concatenate([jnp.dot(blk, w_a, preferred_element_type=jnp.float32) for blk in blocks], axis=0)
            b = jnp.concatenate([jnp.dot(blk, w_b, preferred_element_type=jnp.float32) for blk in blocks], axis=0)
        else:
            a = jnp.dot(h, w_a, preferred_element_type=jnp.float32)
            b = jnp.dot(h, w_b, preferred_element_type=jnp.float32)
        act = (a * jax.nn.sigmoid(a) * b).astype(jnp.bfloat16)
        part = jnp.dot(act, wout_ref[lo:lo + FF_CHUNK, :], preferred_element_type=jnp.float32)
        acc = part if acc is None else acc + part
    o_ref[...] = x + (0.5 * gate) * acc
    for src, dst in zip(cast_src, cast_dst):
        dst[...] = src[...].astype(dst.dtype)


def _ffn(x2d, mod, g, w_in, w_out, *, mod_row, tiles_per_batch, cast=()):
    n_rows = x2d.shape[0]
    n_steps = n_rows // FFN_ROW_TILE
    cast_shapes = [arr.shape for arr in cast]
    cast, cast_specs = _cast_specs(cast, n_steps)
    out = pl.pallas_call(
        functools.partial(_ffn_kernel, mod_row=mod_row, n_cast=len(cast)),
        grid=(n_steps,),
        in_specs=[
            pl.BlockSpec((FFN_ROW_TILE, D_MODEL), lambda i: (i, 0)),
            pl.BlockSpec((1, N_ADA, D_MODEL), lambda i: (i // tiles_per_batch, 0, 0)),
            _resident((1, D_MODEL)),
            _resident((D_MODEL, 2 * D_FF)),
            _resident((D_FF, D_MODEL)),
            *cast_specs,
        ],
        out_specs=[pl.BlockSpec((FFN_ROW_TILE, D_MODEL), lambda i: (i, 0)), *cast_specs],
        out_shape=[jax.ShapeDtypeStruct(x2d.shape, x2d.dtype),
                   *[jax.ShapeDtypeStruct(arr.shape, jnp.bfloat16) for arr in cast]],
        compiler_params=pltpu.CompilerParams(
            dimension_semantics=("arbitrary",), vmem_limit_bytes=VMEM_LIMIT_BYTES),
        name="swiglu_sublayer",
    )(x2d, mod, g.reshape(1, D_MODEL), w_in, w_out, *cast)
    return out[0], [arr.reshape(shape) for arr, shape in zip(out[1:], cast_shapes)]


def _head_rms_norm(t, head_gain):
    lane = lax.broadcasted_iota(jnp.int32, (t.shape[0], LANES), 1)
    low = lane < HEAD_DIM
    pair_gain = jnp.concatenate([head_gain, head_gain], axis=1)
    parts = []
    for p in range(HEAD_PAIRS):
        blk = t[:, p * LANES:(p + 1) * LANES]
        sq = blk * blk
        r_low = lax.rsqrt(jnp.sum(jnp.where(low, sq, 0.0), axis=-1, keepdims=True) * (1.0 / HEAD_DIM) + EPS)
        r_high = lax.rsqrt(jnp.sum(jnp.where(low, 0.0, sq), axis=-1, keepdims=True) * (1.0 / HEAD_DIM) + EPS)
        scaled = blk * jnp.where(low, r_low, r_high) * pair_gain
        parts.append(scaled.astype(jnp.bfloat16))
    return jnp.concatenate(parts, axis=-1)


def _build_bias_table(diag_ref, bias_buf):
    r = lax.broadcasted_iota(jnp.int32, (Q_BLOCK, BIAS_PERIOD), 0)
    j = lax.broadcasted_iota(jnp.int32, (Q_BLOCK, BIAS_PERIOD), 1)
    rel_chunk = r // CHUNK + LEFT_CHUNKS - j // CHUNK
    in_band = jnp.logical_and(rel_chunk >= 0, rel_chunk <= LEFT_CHUNKS)
    for head in range(N_HEADS):
        diag = jnp.broadcast_to(diag_ref[head:head + 1, :], (Q_BLOCK, BIAS_PERIOD))
        table = pltpu.roll(diag, 0, 1, stride=1, stride_axis=0)
        table = jnp.where(in_band, table * LOG2_E, MASK_VALUE)
        row0 = (head % 2) * Q_BLOCK
        bias_buf[head // 2, row0:row0 + Q_BLOCK, :] = table[:, :KEY_WINDOW]
    bias_buf[HEAD_PAIRS:2 * HEAD_PAIRS] = jnp.full((HEAD_PAIRS, 2 * Q_BLOCK, KEY_WINDOW), MASK_VALUE,
                                                   bias_buf.dtype)


def _mix_kernel(x_ref, mod_ref, g_ref, win_ref, qg_ref, kg_ref, diag_ref, wa_ref, wg_ref,
                ps_ref, wp_ref, wo_ref, o_ref, k_buf, v_buf, u_buf, a_buf, bias_buf, wpool_buf):
    tile = pl.program_id(1)
    first = tile == 0

    @pl.when(jnp.logical_and(pl.program_id(0) == 0, first))
    def _():
        _build_bias_table(diag_ref, bias_buf)
        for gi in range(POOL_GROUPS):
            rows = slice(gi * POOL_GROUP_DIM, (gi + 1) * POOL_GROUP_DIM)
            scaled = (wg_ref[rows, :].astype(jnp.float32) * ps_ref[:, rows]).astype(jnp.bfloat16)
            wpool_buf[rows, :] = jnp.dot(scaled, wp_ref[rows, :],
                                         preferred_element_type=jnp.float32).astype(wpool_buf.dtype)
        for p in range(HEAD_PAIRS):
            v_buf[ROW_TILE:2 * ROW_TILE, p * V_GROUP + LANES:(p + 1) * V_GROUP] = jnp.ones(
                (ROW_TILE, LANES), v_buf.dtype)

    @pl.when(first)
    def _():
        k_buf[0:ROW_TILE, :] = jnp.zeros((ROW_TILE, ATTN_WIDTH), k_buf.dtype)
        v_buf[0:ROW_TILE, :] = jnp.zeros((ROW_TILE, HEAD_PAIRS * V_GROUP), v_buf.dtype)
        u_buf[0:POOL_HISTORY, :] = jnp.zeros((POOL_HISTORY, POOL_WIDTH), u_buf.dtype)

    x = x_ref[...]
    shift = mod_ref[0, 3:4, :]
    scale = mod_ref[0, 4:5, :]
    gate = mod_ref[0, 5:6, :]
    h = _norm_modulate(x, g_ref[...], shift, scale)

    def proj(col0, width):
        return jnp.dot(h, win_ref[:, col0:col0 + width], preferred_element_type=jnp.float32)

    u = proj(3 * ATTN_WIDTH, POOL_WIDTH)
    u_buf[POOL_HISTORY:POOL_HISTORY + ROW_TILE, :] = u
    pos = tile * ROW_TILE + lax.broadcasted_iota(jnp.int32, (ROW_TILE, 1), 0)
    sums = u_buf[...]
    pooled = []
    span = 1
    for gi, window in enumerate(POOL_WINDOWS):
        while span < window:
            sums = sums + pltpu.roll(sums, span, axis=0)
            span *= 2
        total = sums[POOL_HISTORY:, :POOL_GROUP_DIM]
        if gi + 1 < POOL_GROUPS:
            sums = sums[:, POOL_GROUP_DIM:]
        count = jnp.minimum(pos + 1, window).astype(jnp.float32)
        u_group = u[:, gi * POOL_GROUP_DIM:(gi + 1) * POOL_GROUP_DIM]
        pooled.append((total / count - u_group).astype(jnp.bfloat16))
    pooled = jnp.concatenate(pooled, axis=-1)
    u_buf[0:POOL_HISTORY, :] = u_buf[ROW_TILE:ROW_TILE + POOL_HISTORY, :]

    q = proj(0, ATTN_WIDTH)
    qn = _head_rms_norm(q, qg_ref[...] * (HEAD_DIM ** -0.5 * LOG2_E))
    k = proj(ATTN_WIDTH, ATTN_WIDTH)
    k_buf[ROW_TILE:2 * ROW_TILE, :] = _head_rms_norm(k, kg_ref[...])
    v = proj(2 * ATTN_WIDTH, ATTN_WIDTH)
    for p in range(HEAD_PAIRS):
        v_buf[ROW_TILE:2 * ROW_TILE, p * V_GROUP:p * V_GROUP + LANES] = (
            v[:, p * LANES:(p + 1) * LANES].astype(jnp.bfloat16))

    lane = lax.broadcasted_iota(jnp.int32, (Q_BLOCK, LANES), 1)
    low = lane < HEAD_DIM
    masked_table = first.astype(jnp.int32) * HEAD_PAIRS

    def scores(half, p):
        base = half * QK_ROWS
        cols = slice(p * LANES, (p + 1) * LANES)
        stack = []
        for blk in range(QK_ROWS // Q_BLOCK):
            q_pair = qn[base + blk * Q_BLOCK:base + (blk + 1) * Q_BLOCK, cols]
            zero = jnp.zeros_like(q_pair)
            stack += [jnp.where(low, q_pair, zero), jnp.where(low, zero, q_pair)]
        return lax.dot_general(jnp.concatenate(stack, axis=0), k_buf[base:base + QK_SPAN, cols],
                               (((1,), (1,)), ((), ())),
                               preferred_element_type=jnp.float32)

    def attend(half, p, s_all):
        cols = slice(p * LANES, (p + 1) * LANES)
        for blk in range(QK_ROWS // Q_BLOCK):
            row0 = half * QK_ROWS + blk * Q_BLOCK
            bias = jnp.concatenate(
                [bias_buf[(masked_table if c0 < ROW_TILE - row0 else 0) + p, :, c0:c0 + LANES]
                 for c0 in range(0, KEY_WINDOW, LANES)], axis=1)
            s = s_all[2 * blk * Q_BLOCK:2 * (blk + 1) * Q_BLOCK,
                      blk * Q_BLOCK:blk * Q_BLOCK + KEY_WINDOW] + bias
            m = jnp.max(s, axis=-1, keepdims=True)
            pexp = jnp.exp2(s - m).astype(jnp.bfloat16)
            o = jnp.dot(pexp, v_buf[row0:row0 + KEY_WINDOW, p * V_GROUP:(p + 1) * V_GROUP],
                        preferred_element_type=jnp.float32)
            o = o[:, :LANES] / o[:, LANES:]
            a_buf[row0:row0 + Q_BLOCK, cols] = jnp.where(
                low, o[:Q_BLOCK], o[Q_BLOCK:]).astype(a_buf.dtype)

    units = [(half, p) for half in range(ROW_TILE // QK_ROWS) for p in range(HEAD_PAIRS)]
    gate_col0 = 3 * ATTN_WIDTH + POOL_WIDTH
    gate_width = 2 * D_MODEL // len(units)
    gate_parts = [proj(gate_col0, gate_width)]
    s_next = scores(*units[0])
    for n, unit in enumerate(units):
        s_cur = s_next
        if n + 1 < len(units):
            s_next = scores(*units[n + 1])
            gate_parts.append(proj(gate_col0 + (n + 1) * gate_width, gate_width))
        attend(*unit, s_cur)
    gates = jnp.concatenate(gate_parts, axis=1)
    ya = jnp.dot(a_buf[...], wa_ref[...], preferred_element_type=jnp.float32)
    k_buf[0:ROW_TILE, :] = k_buf[ROW_TILE:2 * ROW_TILE, :]
    v_buf[0:ROW_TILE, :] = v_buf[ROW_TILE:2 * ROW_TILE, :]

    yb = jnp.dot(pooled, wpool_buf[...], preferred_element_type=jnp.float32)

    merged = (jax.nn.sigmoid(gates[:, :D_MODEL]) * ya
              + jax.nn.sigmoid(gates[:, D_MODEL:]) * yb).astype(jnp.bfloat16)
    o_ref[...] = x + gate * jnp.dot(merged, wo_ref[...], preferred_element_type=jnp.float32)


def _bias_diagonals(rel_bias):
    n_far = BAND - REL_CLIP + 1
    n_near = KEY_WINDOW - n_far
    far = rel_bias[:, 2 * REL_CLIP:]
    near = rel_bias[:, 2 * REL_CLIP - n_near:2 * REL_CLIP][:, ::-1]
    return jnp.concatenate(
        [jnp.broadcast_to(far, (N_HEADS, n_far)), near,
         jnp.broadcast_to(far, (N_HEADS, BIAS_PERIOD - KEY_WINDOW))], axis=1).astype(jnp.float32)


def _token_mix(x3d, mod, g, w_in, q_gain, k_gain, rel_bias, w_attn_out, w_pool_group, pool_scale,
               w_pool_out, w_o):
    n_batch, seq, _ = x3d.shape
    bf16 = jnp.bfloat16
    in_cols = w_in.shape[1]
    return pl.pallas_call(
        _mix_kernel,
        grid=(n_batch, seq // ROW_TILE),
        in_specs=[
            pl.BlockSpec((None, ROW_TILE, D_MODEL), lambda b, i: (b, i, 0)),
            pl.BlockSpec((1, N_ADA, D_MODEL), lambda b, i: (b, 0, 0)),
            _resident((1, D_MODEL)),
            _resident((D_MODEL, in_cols)),
            _resident((1, HEAD_DIM)),
            _resident((1, HEAD_DIM)),
            _resident((N_HEADS, BIAS_PERIOD)),
            _resident((ATTN_WIDTH, D_MODEL)),
            _resident((POOL_WIDTH, POOL_GROUP_DIM)),
            _resident((1, POOL_WIDTH)),
            _resident((POOL_WIDTH, D_MODEL)),
            _resident((D_MODEL, D_MODEL)),
        ],
        out_specs=pl.BlockSpec((None, ROW_TILE, D_MODEL), lambda b, i: (b, i, 0)),
        out_shape=jax.ShapeDtypeStruct(x3d.shape, x3d.dtype),
        scratch_shapes=[
            pltpu.VMEM((2 * ROW_TILE, ATTN_WIDTH), bf16),
            pltpu.VMEM((2 * ROW_TILE, HEAD_PAIRS * V_GROUP), bf16),
            pltpu.VMEM((POOL_HISTORY + ROW_TILE, POOL_WIDTH), jnp.float32),
            pltpu.VMEM((ROW_TILE, ATTN_WIDTH), bf16),
            pltpu.VMEM((2 * HEAD_PAIRS, 2 * Q_BLOCK, KEY_WINDOW), jnp.float32),
            pltpu.VMEM((POOL_WIDTH, D_MODEL), bf16),
        ],
        compiler_params=pltpu.CompilerParams(
            dimension_semantics=("arbitrary", "arbitrary"), vmem_limit_bytes=VMEM_LIMIT_BYTES),
        name="token_mix_sublayer",
    )(x3d, mod, g.reshape(1, D_MODEL), w_in,
      q_gain.reshape(1, HEAD_DIM), k_gain.reshape(1, HEAD_DIM),
      _bias_diagonals(rel_bias), w_attn_out, w_pool_group, pool_scale.reshape(1, POOL_WIDTH), w_pool_out, w_o)


def kernel(x, c, w_ada, b_ada, g_ffn1, w_ffn1_in, w_ffn1_out, g_mix, w_in, q_gain, k_gain, rel_bias,
           w_attn_out, w_pool_group, pool_scale, w_pool_out, w_o, g_ffn2, w_ffn2_in, w_ffn2_out):
    n_batch, seq, d_model = x.shape
    depth = w_ada.shape[0]
    assert d_model == D_MODEL and seq % ROW_TILE == 0 and seq % FFN_ROW_TILE == 0
    tiles_per_batch = seq // FFN_ROW_TILE
    for l in range(depth):
        mod, (w_ffn1_in_b, w_ffn1_out_b) = _modulation(c, w_ada[l], b_ada[l],
                                                        cast=(w_ffn1_in[l], w_ffn1_out[l]))
        mod = mod.reshape(n_batch, N_ADA, D_MODEL)
        x2d = x.reshape(n_batch * seq, D_MODEL)
        later = (w_in[l], w_attn_out[l], w_pool_group[l].reshape(POOL_WIDTH, POOL_GROUP_DIM),
                 w_pool_out[l], w_o[l], w_ffn2_in[l], w_ffn2_out[l])
        x2d, later = _ffn(x2d, mod, g_ffn1[l], w_ffn1_in_b, w_ffn1_out_b, mod_row=0,
                          tiles_per_batch=tiles_per_batch, cast=later)
        w_in_b, w_attn_out_b, w_pool_group_b, w_pool_out_b, w_o_b, w_ffn2_in_b, w_ffn2_out_b = later
        x = _token_mix(x2d.reshape(n_batch, seq, D_MODEL), mod, g_mix[l], w_in_b, q_gain[l], k_gain[l],
                       rel_bias[l], w_attn_out_b, w_pool_group_b, pool_scale[l], w_pool_out_b, w_o_b)
        x2d, _ = _ffn(x.reshape(n_batch * seq, D_MODEL), mod, g_ffn2[l], w_ffn2_in_b, w_ffn2_out_b,
                      mod_row=6, tiles_per_batch=tiles_per_batch)
        x = x2d.reshape(n_batch, seq, D_MODEL)
    return x
```

```python
import functools
import math

import jax
import jax.numpy as jnp
from jax import lax
from jax.experimental import pallas as pl
from jax.experimental.pallas import tpu as pltpu

D_MODEL = 1024
CHUNK = 64
LEFT_CHUNKS = 8
N_HEADS = 8
HEAD_DIM = 64
ATTN_WIDTH = N_HEADS * HEAD_DIM
POOL_WINDOWS = (2, 4, 8, 16)
POOL_GROUPS = len(POOL_WINDOWS)
POOL_WIDTH = 512
POOL_GROUP_DIM = POOL_WIDTH // POOL_GROUPS
REL_CLIP = 128
D_FF = 2816
N_ADA = 9
EPS = 1e-6
MASK_VALUE = -1e30
LOG2_E = math.log2(math.e)

LANES = 128
MXU_DIM = 256
BF16_SUBLANES = 16
VMEM_LIMIT_BYTES = 56 * 1024 * 1024

ROW_TILE = 512
FFN_ROW_TILE = 1024
FF_CHUNK = MXU_DIM
N_FF_CHUNKS = D_FF // FF_CHUNK
BAND = LEFT_CHUNKS * CHUNK
Q_BLOCK = 2 * CHUNK
KEY_WINDOW = Q_BLOCK + BAND
QK_ROWS = 2 * Q_BLOCK
QK_SPAN = QK_ROWS + BAND
HEAD_PAIRS = ATTN_WIDTH // LANES
V_GROUP = 2 * LANES
BIAS_PERIOD = 768
POOL_HISTORY = 16
MOD_STEPS = 8

assert D_FF % FF_CHUNK == 0
assert ROW_TILE == BAND
assert ROW_TILE % QK_ROWS == 0 and 2 * HEAD_DIM == LANES
assert BIAS_PERIOD >= Q_BLOCK + KEY_WINDOW - 1 and BIAS_PERIOD % LANES == 0
assert POOL_HISTORY >= max(POOL_WINDOWS) - 1 and POOL_HISTORY % 8 == 0
assert all(w & (w - 1) == 0 for w in POOL_WINDOWS) and list(POOL_WINDOWS) == sorted(POOL_WINDOWS)
assert D_MODEL % (MOD_STEPS * 8) == 0


def _resident(shape):
    zeros = (0,) * len(shape)
    return pl.BlockSpec(shape, lambda *_: zeros, pipeline_mode=pl.Buffered(1))


def _row_blockable(arr, n_blocks):
    rows, cols = arr.shape
    split = 1
    while (rows * split) % (n_blocks * BF16_SUBLANES) or (cols // split) % LANES:
        split *= 2
        assert cols % split == 0, (arr.shape, n_blocks)
    return arr.reshape(rows * split, cols // split)


def _cast_specs(cast, n_blocks):
    views = [_row_blockable(arr, n_blocks) for arr in cast]
    specs = [pl.BlockSpec((v.shape[0] // n_blocks, v.shape[1]), lambda i: (i, 0)) for v in views]
    return views, specs


def _mod_kernel(ct_ref, w_ref, b_ref, *rest, n_cast):
    cast_src, o_ref, cast_dst = rest[:n_cast], rest[n_cast], rest[n_cast + 1:]

    n_batch, n_ada, width = o_ref.shape

    @pl.when(pl.program_id(0) == 0)
    def _():
        for j in range(n_ada):
            o_ref[:, j, :] = jnp.broadcast_to(b_ref[:, j * width:(j + 1) * width], (n_batch, width))

    for b in range(n_batch):
        col = ct_ref[:, b:b + 1]
        col = col * jax.nn.sigmoid(col)
        total = jnp.sum(w_ref[...] * col, axis=0, keepdims=True)
        for j in range(n_ada):
            o_ref[b, j:j + 1, :] += total[:, j * width:(j + 1) * width]
    for src, dst in zip(cast_src, cast_dst):
        dst[...] = src[...].astype(dst.dtype)


def _modulation(c, w_ada, b_ada, cast=()):
    n_batch = c.shape[0]
    n_cols = w_ada.shape[1]
    rows = D_MODEL // MOD_STEPS
    cast_shapes = [arr.shape for arr in cast]
    cast, cast_specs = _cast_specs(cast, MOD_STEPS)
    out = pl.pallas_call(
        functools.partial(_mod_kernel, n_cast=len(cast)),
        grid=(MOD_STEPS,),
        in_specs=[
            pl.BlockSpec((rows, n_batch), lambda j: (j, 0)),
            pl.BlockSpec((rows, n_cols), lambda j: (j, 0)),
            pl.BlockSpec((1, n_cols), lambda j: (0, 0)),
            *cast_specs,
        ],
        out_specs=[pl.BlockSpec((n_batch, N_ADA, D_MODEL), lambda j: (0, 0, 0)), *cast_specs],
        out_shape=[jax.ShapeDtypeStruct((n_batch, N_ADA, D_MODEL), jnp.float32),
                   *[jax.ShapeDtypeStruct(arr.shape, jnp.bfloat16) for arr in cast]],
        compiler_params=pltpu.CompilerParams(
            dimension_semantics=("arbitrary",), vmem_limit_bytes=VMEM_LIMIT_BYTES),
        name="adaln_modulation",
    )(c.T, w_ada, b_ada.reshape(1, n_cols), *cast)
    return out[0], [arr.reshape(shape) for arr, shape in zip(out[1:], cast_shapes)]


def _norm_modulate(x, g, shift, scale):
    ms = jnp.mean(x * x, axis=-1, keepdims=True)
    col_gain = g * (1.0 + scale)
    return (x * lax.rsqrt(ms + EPS) * col_gain + shift).astype(jnp.bfloat16)


def _ffn_kernel(x_ref, mod_ref, g_ref, win_ref, wout_ref, *rest, mod_row, n_cast):
    cast_src, o_ref, cast_dst = rest[:n_cast], rest[n_cast], rest[n_cast + 1:]
    x = x_ref[...]
    shift = mod_ref[0, mod_row:mod_row + 1, :]
    scale = mod_ref[0, mod_row + 1:mod_row + 2, :]
    gate = mod_ref[0, mod_row + 2:mod_row + 3, :]
    h = _norm_modulate(x, g_ref[...], shift, scale)
    acc = None
    for c in range(N_FF_CHUNKS):
        lo = c * FF_CHUNK
        a = jnp.dot(h, win_ref[:, lo:lo + FF_CHUNK], preferred_element_type=jnp.float32)
        b = jnp.dot(h, win_ref[:, D_FF + lo:D_FF + lo + FF_CHUNK], preferred_element_type=jnp.float32)
        act = (a * jax.nn.sigmoid(a) * b).astype(jnp.bfloat16)
        part = jnp.dot(act, wout_ref[lo:lo + FF_CHUNK, :], preferred_element_type=jnp.float32)
        acc = part if acc is None else acc + part
    o_ref[...] = x + (0.5 * gate) * acc
    for src, dst in zip(cast_src, cast_dst):
        dst[...] = src[...].astype(dst.dtype)


def _ffn(x2d, mod, g, w_in, w_out, *, mod_row, tiles_per_batch, cast=()):
    n_rows = x2d.shape[0]
    n_steps = n_rows // FFN_ROW_TILE
    cast_shapes = [arr.shape for arr in cast]
    cast, cast_specs = _cast_specs(cast, n_steps)
    out = pl.pallas_call(
        functools.partial(_ffn_kernel, mod_row=mod_row, n_cast=len(cast)),
        grid=(n_steps,),
        in_specs=[
            pl.BlockSpec((FFN_ROW_TILE, D_MODEL), lambda i: (i, 0)),
            pl.BlockSpec((1, N_ADA, D_MODEL), lambda i: (i // tiles_per_batch, 0, 0)),
            _resident((1, D_MODEL)),
            _resident((D_MODEL, 2 * D_FF)),
            _resident((D_FF, D_MODEL)),
            *cast_specs,
        ],
        out_specs=[pl.BlockSpec((FFN_ROW_TILE, D_MODEL), lambda i: (i, 0)), *cast_specs],
        out_shape=[jax.ShapeDtypeStruct(x2d.shape, x2d.dtype),
                   *[jax.ShapeDtypeStruct(arr.shape, jnp.bfloat16) for arr in cast]],
        compiler_params=pltpu.CompilerParams(
            dimension_semantics=("arbitrary",), vmem_limit_bytes=VMEM_LIMIT_BYTES),
        name="swiglu_sublayer",
    )(x2d, mod, g.reshape(1, D_MODEL), w_in, w_out, *cast)
    return out[0], [arr.reshape(shape) for arr, shape in zip(out[1:], cast_shapes)]


def _head_rms_norm(t, head_gain):
    lane = lax.broadcasted_iota(jnp.int32, (t.shape[0], LANES), 1)
    low = lane < HEAD_DIM
    pair_gain = jnp.concatenate([head_gain, head_gain], axis=1)
    parts = []
    for p in range(HEAD_PAIRS):
        blk = t[:, p * LANES:(p + 1) * LANES]
        sq = blk * blk
        r_low = lax.rsqrt(jnp.sum(jnp.where(low, sq, 0.0), axis=-1, keepdims=True) * (1.0 / HEAD_DIM) + EPS)
        r_high = lax.rsqrt(jnp.sum(jnp.where(low, 0.0, sq), axis=-1, keepdims=True) * (1.0 / HEAD_DIM) + EPS)
        scaled = blk * jnp.where(low, r_low, r_high) * pair_gain
        parts.append(scaled.astype(jnp.bfloat16))
    return jnp.concatenate(parts, axis=-1)


def _build_bias_table(diag_ref, bias_buf):
    r = lax.broadcasted_iota(jnp.int32, (Q_BLOCK, BIAS_PERIOD), 0)
    j = lax.broadcasted_iota(jnp.int32, (Q_BLOCK, BIAS_PERIOD), 1)
    rel_chunk = r // CHUNK + LEFT_CHUNKS - j // CHUNK
    in_band = jnp.logical_and(rel_chunk >= 0, rel_chunk <= LEFT_CHUNKS)
    for head in range(N_HEADS):
        diag = jnp.broadcast_to(diag_ref[head:head + 1, :], (Q_BLOCK, BIAS_PERIOD))
        table = pltpu.roll(diag, 0, 1, stride=1, stride_axis=0)
        table = jnp.where(in_band, table * LOG2_E, MASK_VALUE)
        row0 = (head % 2) * Q_BLOCK
        bias_buf[head // 2, row0:row0 + Q_BLOCK, :] = table[:, :KEY_WINDOW]
    bias_buf[HEAD_PAIRS:2 * HEAD_PAIRS] = jnp.full((HEAD_PAIRS, 2 * Q_BLOCK, KEY_WINDOW), MASK_VALUE,
                                                   bias_buf.dtype)


def _mix_kernel(x_ref, mod_ref, g_ref, win_ref, qg_ref, kg_ref, diag_ref, wa_ref, wg_ref,
                ps_ref, wp_ref, wo_ref, o_ref, k_buf, v_buf, u_buf, a_buf, bias_buf, wpool_buf):
    tile = pl.program_id(1)
    first = tile == 0

    @pl.when(jnp.logical_and(pl.program_id(0) == 0, first))
    def _():
        _build_bias_table(diag_ref, bias_buf)
        for gi in range(POOL_GROUPS):
            rows = slice(gi * POOL_GROUP_DIM, (gi + 1) * POOL_GROUP_DIM)
            scaled = (wg_ref[rows, :].astype(jnp.float32) * ps_ref[:, rows]).astype(jnp.bfloat16)
            wpool_buf[rows, :] = jnp.dot(scaled, wp_ref[rows, :],
                                         preferred_element_type=jnp.float32).astype(wpool_buf.dtype)
        for p in range(HEAD_PAIRS):
            v_buf[ROW_TILE:2 * ROW_TILE, p * V_GROUP + LANES:(p + 1) * V_GROUP] = jnp.ones(
                (ROW_TILE, LANES), v_buf.dtype)

    @pl.when(first)
    def _():
        k_buf[0:ROW_TILE, :] = jnp.zeros((ROW_TILE, ATTN_WIDTH), k_buf.dtype)
        v_buf[0:ROW_TILE, :] = jnp.zeros((ROW_TILE, HEAD_PAIRS * V_GROUP), v_buf.dtype)
        u_buf[0:POOL_HISTORY, :] = jnp.zeros((POOL_HISTORY, POOL_WIDTH), u_buf.dtype)

    x = x_ref[...]
    shift = mod_ref[0, 3:4, :]
    scale = mod_ref[0, 4:5, :]
    gate = mod_ref[0, 5:6, :]
    h = _norm_modulate(x, g_ref[...], shift, scale)

    def proj(col0, width):
        return jnp.dot(h, win_ref[:, col0:col0 + width], preferred_element_type=jnp.float32)

    u = proj(3 * ATTN_WIDTH, POOL_WIDTH)
    u_buf[POOL_HISTORY:POOL_HISTORY + ROW_TILE, :] = u
    pos = tile * ROW_TILE + lax.broadcasted_iota(jnp.int32, (ROW_TILE, 1), 0)
    sums = u_buf[...]
    pooled = []
    span = 1
    for gi, window in enumerate(POOL_WINDOWS):
        while span < window:
            sums = sums + pltpu.roll(sums, span, axis=0)
            span *= 2
        total = sums[POOL_HISTORY:, :POOL_GROUP_DIM]
        if gi + 1 < POOL_GROUPS:
            sums = sums[:, POOL_GROUP_DIM:]
        count = jnp.minimum(pos + 1, window).astype(jnp.float32)
        u_group = u[:, gi * POOL_GROUP_DIM:(gi + 1) * POOL_GROUP_DIM]
        pooled.append((total / count - u_group).astype(jnp.bfloat16))
    pooled = jnp.concatenate(pooled, axis=-1)
    u_buf[0:POOL_HISTORY, :] = u_buf[ROW_TILE:ROW_TILE + POOL_HISTORY, :]

    q = proj(0, ATTN_WIDTH)
    qn = _head_rms_norm(q, qg_ref[...] * (HEAD_DIM ** -0.5 * LOG2_E))
    k = proj(ATTN_WIDTH, ATTN_WIDTH)
    k_buf[ROW_TILE:2 * ROW_TILE, :] = _head_rms_norm(k, kg_ref[...])
    v = proj(2 * ATTN_WIDTH, ATTN_WIDTH)
    for p in range(HEAD_PAIRS):
        v_buf[ROW_TILE:2 * ROW_TILE, p * V_GROUP:p * V_GROUP + LANES] = (
            v[:, p * LANES:(p + 1) * LANES].astype(jnp.bfloat16))

    lane = lax.broadcasted_iota(jnp.int32, (Q_BLOCK, LANES), 1)
    low = lane < HEAD_DIM
    masked_table = first.astype(jnp.int32) * HEAD_PAIRS

    def scores(half, p):
        base = half * QK_ROWS
        cols = slice(p * LANES, (p + 1) * LANES)
        stack = []
        for blk in range(QK_ROWS // Q_BLOCK):
            q_pair = qn[base + blk * Q_BLOCK:base + (blk + 1) * Q_BLOCK, cols]
            zero = jnp.zeros_like(q_pair)
            stack += [jnp.where(low, q_pair, zero), jnp.where(low, zero, q_pair)]
        return lax.dot_general(jnp.concatenate(stack, axis=0), k_buf[base:base + QK_SPAN, cols],
                               (((1,), (1,)), ((), ())),
                               preferred_element_type=jnp.float32)

    def attend(half, p, s_all):
        cols = slice(p * LANES, (p + 1) * LANES)
        for blk in range(QK_ROWS // Q_BLOCK):
            row0 = half * QK_ROWS + blk * Q_BLOCK
            bias = jnp.concatenate(
                [bias_buf[(masked_table if c0 < ROW_TILE - row0 else 0) + p, :, c0:c0 + LANES]
                 for c0 in range(0, KEY_WINDOW, LANES)], axis=1)
            s = s_all[2 * blk * Q_BLOCK:2 * (blk + 1) * Q_BLOCK,
                      blk * Q_BLOCK:blk * Q_BLOCK + KEY_WINDOW] + bias
            m = jnp.max(s, axis=-1, keepdims=True)
            pexp = jnp.exp2(s - m).astype(jnp.bfloat16)
            o = jnp.dot(pexp, v_buf[row0:row0 + KEY_WINDOW, p * V_GROUP:(p + 1) * V_GROUP],
                        preferred_element_type=jnp.float32)
            o = o[:, :LANES] / o[:, LANES:]
            a_buf[row0:row0 + Q_BLOCK, cols] = jnp.where(
                low, o[:Q_BLOCK], o[Q_BLOCK:]).astype(a_buf.dtype)

    units = [(half, p) for half in range(ROW_TILE // QK_ROWS) for p in range(HEAD_PAIRS)]
    gate_col0 = 3 * ATTN_WIDTH + POOL_WIDTH
    gate_width = 2 * D_MODEL // len(units)
    gate_parts = [proj(gate_col0, gate_width)]
    s_next = scores(*units[0])
    for n, unit in enumerate(units):
        s_cur = s_next
        if n + 1 < len(units):
            s_next = scores(*units[n + 1])
            gate_parts.append(proj(gate_col0 + (n + 1) * gate_width, gate_width))
        attend(*unit, s_cur)
    gates = jnp.concatenate(gate_parts, axis=1)
    ya = jnp.dot(a_buf[...], wa_ref[...], preferred_element_type=jnp.float32)
    k_buf[0:ROW_TILE, :] = k_buf[ROW_TILE:2 * ROW_TILE, :]
    v_buf[0:ROW_TILE, :] = v_buf[ROW_TILE:2 * ROW_TILE, :]

    yb = jnp.dot(pooled, wpool_buf[...], preferred_element_type=jnp.float32)

    merged = (jax.nn.sigmoid(gates[:, :D_MODEL]) * ya
              + jax.nn.sigmoid(gates[:, D_MODEL:]) * yb).astype(jnp.bfloat16)
    o_ref[...] = x + gate * jnp.dot(merged, wo_ref[...], preferred_element_type=jnp.float32)


def _bias_diagonals(rel_bias):
    n_far = BAND - REL_CLIP + 1
    n_near = KEY_WINDOW - n_far
    far = rel_bias[:, 2 * REL_CLIP:]
    near = rel_bias[:, 2 * REL_CLIP - n_near:2 * REL_CLIP][:, ::-1]
    return jnp.concatenate(
        [jnp.broadcast_to(far, (N_HEADS, n_far)), near,
         jnp.broadcast_to(far, (N_HEADS, BIAS_PERIOD - KEY_WINDOW))], axis=1).astype(jnp.float32)


def _token_mix(x3d, mod, g, w_in, q_gain, k_gain, rel_bias, w_attn_out, w_pool_group, pool_scale,
               w_pool_out, w_o):
    n_batch, seq, _ = x3d.shape
    bf16 = jnp.bfloat16
    in_cols = w_in.shape[1]
    return pl.pallas_call(
        _mix_kernel,
        grid=(n_batch, seq // ROW_TILE),
        in_specs=[
            pl.BlockSpec((None, ROW_TILE, D_MODEL), lambda b, i: (b, i, 0)),
            pl.BlockSpec((1, N_ADA, D_MODEL), lambda b, i: (b, 0, 0)),
            _resident((1, D_MODEL)),
            _resident((D_MODEL, in_cols)),
            _resident((1, HEAD_DIM)),
            _resident((1, HEAD_DIM)),
            _resident((N_HEADS, BIAS_PERIOD)),
            _resident((ATTN_WIDTH, D_MODEL)),
            _resident((POOL_WIDTH, POOL_GROUP_DIM)),
            _resident((1, POOL_WIDTH)),
            _resident((POOL_WIDTH, D_MODEL)),
            _resident((D_MODEL, D_MODEL)),
        ],
        out_specs=pl.BlockSpec((None, ROW_TILE, D_MODEL), lambda b, i: (b, i, 0)),
        out_shape=jax.ShapeDtypeStruct(x3d.shape, x3d.dtype),
        scratch_shapes=[
            pltpu.VMEM((2 * ROW_TILE, ATTN_WIDTH), bf16),
            pltpu.VMEM((2 * ROW_TILE, HEAD_PAIRS * V_GROUP), bf16),
            pltpu.VMEM((POOL_HISTORY + ROW_TILE, POOL_WIDTH), jnp.float32),
            pltpu.VMEM((ROW_TILE, ATTN_WIDTH), bf16),
            pltpu.VMEM((2 * HEAD_PAIRS, 2 * Q_BLOCK, KEY_WINDOW), jnp.float32),
            pltpu.VMEM((POOL_WIDTH, D_MODEL), bf16),
        ],
        compiler_params=pltpu.CompilerParams(
            dimension_semantics=("arbitrary", "arbitrary"), vmem_limit_bytes=VMEM_LIMIT_BYTES),
        name="token_mix_sublayer",
    )(x3d, mod, g.reshape(1, D_MODEL), w_in,
      q_gain.reshape(1, HEAD_DIM), k_gain.reshape(1, HEAD_DIM),
      _bias_diagonals(rel_bias), w_attn_out, w_pool_group, pool_scale.reshape(1, POOL_WIDTH), w_pool_out, w_o)


def kernel(x, c, w_ada, b_ada, g_ffn1, w_ffn1_in, w_ffn1_out, g_mix, w_in, q_gain, k_gain, rel_bias,
           w_attn_out, w_pool_group, pool_scale, w_pool_out, w_o, g_ffn2, w_ffn2_in, w_ffn2_out):
    n_batch, seq, d_model = x.shape
    depth = w_ada.shape[0]
    assert d_model == D_MODEL and seq % ROW_TILE == 0 and seq % FFN_ROW_TILE == 0
    tiles_per_batch = seq // FFN_ROW_TILE
    for l in range(depth):
        mod, (w_ffn1_in_b, w_ffn1_out_b) = _modulation(c, w_ada[l], b_ada[l],
                                                        cast=(w_ffn1_in[l], w_ffn1_out[l]))
        x2d = x.reshape(n_batch * seq, D_MODEL)
        later = (w_in[l], w_attn_out[l], w_pool_group[l].reshape(POOL_WIDTH, POOL_GROUP_DIM),
                 w_pool_out[l], w_o[l], w_ffn2_in[l], w_ffn2_out[l])
        x2d, later = _ffn(x2d, mod, g_ffn1[l], w_ffn1_in_b, w_ffn1_out_b, mod_row=0,
                          tiles_per_batch=tiles_per_batch, cast=later)
        w_in_b, w_attn_out_b, w_pool_group_b, w_pool_out_b, w_o_b, w_ffn2_in_b, w_ffn2_out_b = later
        x = _token_mix(x2d.reshape(n_batch, seq, D_MODEL), mod, g_mix[l], w_in_b, q_gain[l], k_gain[l],
                       rel_bias[l], w_attn_out_b, w_pool_group_b, pool_scale[l], w_pool_out_b, w_o_b)
        x2d, _ = _ffn(x.reshape(n_batch * seq, D_MODEL), mod, g_ffn2[l], w_ffn2_in_b, w_ffn2_out_b,
                      mod_row=6, tiles_per_batch=tiles_per_batch)
        x = x2d.reshape(n_batch, seq, D_MODEL)
    return x
```

```python
import functools
import math

import jax
import jax.numpy as jnp
from jax import lax
from jax.experimental import pallas as pl
from jax.experimental.pallas import tpu as pltpu

D_MODEL = 1024
CHUNK = 64
LEFT_CHUNKS = 8
N_HEADS = 8
HEAD_DIM = 64
ATTN_WIDTH = N_HEADS * HEAD_DIM
POOL_WINDOWS = (2, 4, 8, 16)
POOL_GROUPS = len(POOL_WINDOWS)
POOL_WIDTH = 512
POOL_GROUP_DIM = POOL_WIDTH // POOL_GROUPS
REL_CLIP = 128
D_FF = 2816
N_ADA = 9
EPS = 1e-6
MASK_VALUE = -1e30
LOG2_E = math.log2(math.e)

LANES = 128
MXU_DIM = 256
BF16_SUBLANES = 16
VMEM_LIMIT_BYTES = 56 * 1024 * 1024

ROW_TILE = 512
FFN_ROW_TILE = 1024
FFN_STAGE_BYTES = 1024 * 1024
FF_CHUNK = MXU_DIM
N_FF_CHUNKS = D_FF // FF_CHUNK
BAND = LEFT_CHUNKS * CHUNK
Q_BLOCK = 2 * CHUNK
KEY_WINDOW = Q_BLOCK + BAND
QK_ROWS = 2 * Q_BLOCK
QK_SPAN = QK_ROWS + BAND
HEAD_PAIRS = ATTN_WIDTH // LANES
V_GROUP = 2 * LANES
BIAS_PERIOD = 768
POOL_HISTORY = 16
MOD_STEPS = 8

assert D_FF % FF_CHUNK == 0
assert ROW_TILE == BAND
assert ROW_TILE % QK_ROWS == 0 and 2 * HEAD_DIM == LANES
assert BIAS_PERIOD >= Q_BLOCK + KEY_WINDOW - 1 and BIAS_PERIOD % LANES == 0
assert POOL_HISTORY >= max(POOL_WINDOWS) - 1 and POOL_HISTORY % 8 == 0
assert all(w & (w - 1) == 0 for w in POOL_WINDOWS) and list(POOL_WINDOWS) == sorted(POOL_WINDOWS)
assert D_MODEL % (MOD_STEPS * 8) == 0


def _resident(shape):
    zeros = (0,) * len(shape)
    return pl.BlockSpec(shape, lambda *_: zeros, pipeline_mode=pl.Buffered(1))


def _row_blockable(arr, n_blocks):
    rows, cols = arr.shape
    split = 1
    while (rows * split) % (n_blocks * BF16_SUBLANES) or (cols // split) % LANES:
        split *= 2
        assert cols % split == 0, (arr.shape, n_blocks)
    return arr.reshape(rows * split, cols // split)


def _cast_specs(cast, n_blocks):
    views = [_row_blockable(arr, n_blocks) for arr in cast]
    specs = [pl.BlockSpec((v.shape[0] // n_blocks, v.shape[1]), lambda i: (i, 0)) for v in views]
    return views, specs


def _mod_kernel(ct_ref, w_ref, b_ref, *rest, n_cast):
    cast_src, o_ref, cast_dst = rest[:n_cast], rest[n_cast], rest[n_cast + 1:]

    n_batch, n_ada, width = o_ref.shape

    @pl.when(pl.program_id(0) == 0)
    def _():
        for j in range(n_ada):
            o_ref[:, j, :] = jnp.broadcast_to(b_ref[:, j * width:(j + 1) * width], (n_batch, width))

    for b in range(n_batch):
        col = ct_ref[:, b:b + 1]
        col = col * jax.nn.sigmoid(col)
        total = jnp.sum(w_ref[...] * col, axis=0, keepdims=True)
        for j in range(n_ada):
            o_ref[b, j:j + 1, :] += total[:, j * width:(j + 1) * width]
    for src, dst in zip(cast_src, cast_dst):
        dst[...] = src[...].astype(dst.dtype)


def _modulation(c, w_ada, b_ada, cast=()):
    n_batch = c.shape[0]
    n_cols = w_ada.shape[1]
    rows = D_MODEL // MOD_STEPS
    cast_shapes = [arr.shape for arr in cast]
    cast, cast_specs = _cast_specs(cast, MOD_STEPS)
    out = pl.pallas_call(
        functools.partial(_mod_kernel, n_cast=len(cast)),
        grid=(MOD_STEPS,),
        in_specs=[
            pl.BlockSpec((rows, n_batch), lambda j: (j, 0)),
            pl.BlockSpec((rows, n_cols), lambda j: (j, 0)),
            pl.BlockSpec((1, n_cols), lambda j: (0, 0)),
            *cast_specs,
        ],
        out_specs=[pl.BlockSpec((n_batch, N_ADA, D_MODEL), lambda j: (0, 0, 0)), *cast_specs],
        out_shape=[jax.ShapeDtypeStruct((n_batch, N_ADA, D_MODEL), jnp.float32),
                   *[jax.ShapeDtypeStruct(arr.shape, jnp.bfloat16) for arr in cast]],
        compiler_params=pltpu.CompilerParams(
            dimension_semantics=("arbitrary",), vmem_limit_bytes=VMEM_LIMIT_BYTES),
        name="adaln_modulation",
    )(c.T, w_ada, b_ada.reshape(1, n_cols), *cast)
    return out[0], [arr.reshape(shape) for arr, shape in zip(out[1:], cast_shapes)]


def _norm_modulate(x, g, shift, scale):
    ms = jnp.mean(x * x, axis=-1, keepdims=True)
    col_gain = g * (1.0 + scale)
    return (x * lax.rsqrt(ms + EPS) * col_gain + shift).astype(jnp.bfloat16)


def _stage_rows(shape):
    rows, cols = shape
    fits = [r for r in range(BF16_SUBLANES, rows + 1, BF16_SUBLANES)
            if rows % r == 0 and r * cols * 4 <= FFN_STAGE_BYTES]
    return max(fits)


def _stage_copy(src_hbm, stage, sems, chunk, slot):
    rows = stage.shape[1]
    return pltpu.make_async_copy(src_hbm.at[pl.ds(chunk * rows, rows), :], stage.at[slot], sems.at[slot])


def _load_as_bf16(src_hbm, dst, stage, sems):
    rows = stage.shape[1]
    n_chunks = src_hbm.shape[0] // rows
    _stage_copy(src_hbm, stage, sems, 0, 0).start()
    for c in range(n_chunks):
        slot = c % 2
        if c + 1 < n_chunks:
            _stage_copy(src_hbm, stage, sems, c + 1, 1 - slot).start()
        _stage_copy(src_hbm, stage, sems, c, slot).wait()
        dst[c * rows:(c + 1) * rows, :] = stage[slot].astype(dst.dtype)


def _ffn_kernel(x_ref, mod_ref, g_ref, win_ref, wout_ref, *rest, mod_row, n_cast, staged):
    cast_src, o_ref, cast_dst = rest[:n_cast], rest[n_cast], rest[n_cast + 1:2 * n_cast + 1]
    if staged:
        win_hbm, wout_hbm = win_ref, wout_ref
        win_ref, wout_ref, win_stage, wout_stage, win_sems, wout_sems = rest[2 * n_cast + 1:]

        @pl.when(pl.program_id(0) == 0)
        def _():
            _load_as_bf16(win_hbm, win_ref, win_stage, win_sems)
            _load_as_bf16(wout_hbm, wout_ref, wout_stage, wout_sems)

    x = x_ref[...]
    shift = mod_ref[0, mod_row:mod_row + 1, :]
    scale = mod_ref[0, mod_row + 1:mod_row + 2, :]
    gate = mod_ref[0, mod_row + 2:mod_row + 3, :]
    h = _norm_modulate(x, g_ref[...], shift, scale)
    acc = None
    for c in range(N_FF_CHUNKS):
        lo = c * FF_CHUNK
        a = jnp.dot(h, win_ref[:, lo:lo + FF_CHUNK], preferred_element_type=jnp.float32)
        b = jnp.dot(h, win_ref[:, D_FF + lo:D_FF + lo + FF_CHUNK], preferred_element_type=jnp.float32)
        act = (a * jax.nn.sigmoid(a) * b).astype(jnp.bfloat16)
        part = jnp.dot(act, wout_ref[lo:lo + FF_CHUNK, :], preferred_element_type=jnp.float32)
        acc = part if acc is None else acc + part
    o_ref[...] = x + (0.5 * gate) * acc
    for src, dst in zip(cast_src, cast_dst):
        dst[...] = src[...].astype(dst.dtype)


def _ffn(x2d, mod, g, w_in, w_out, *, mod_row, tiles_per_batch, cast=()):
    n_rows = x2d.shape[0]
    n_steps = n_rows // FFN_ROW_TILE
    cast_shapes = [arr.shape for arr in cast]
    cast, cast_specs = _cast_specs(cast, n_steps)
    staged = w_in.dtype != jnp.bfloat16
    assert (w_out.dtype != jnp.bfloat16) == staged
    if staged:
        w_specs = [pl.BlockSpec(memory_space=pl.ANY), pl.BlockSpec(memory_space=pl.ANY)]
        scratch = [
            pltpu.VMEM(w_in.shape, jnp.bfloat16),
            pltpu.VMEM(w_out.shape, jnp.bfloat16),
            pltpu.VMEM((2, _stage_rows(w_in.shape), w_in.shape[1]), w_in.dtype),
            pltpu.VMEM((2, _stage_rows(w_out.shape), w_out.shape[1]), w_out.dtype),
            pltpu.SemaphoreType.DMA((2,)),
            pltpu.SemaphoreType.DMA((2,)),
        ]
    else:
        w_specs = [_resident(w_in.shape), _resident(w_out.shape)]
        scratch = []
    out = pl.pallas_call(
        functools.partial(_ffn_kernel, mod_row=mod_row, n_cast=len(cast), staged=staged),
        grid=(n_steps,),
        in_specs=[
            pl.BlockSpec((FFN_ROW_TILE, D_MODEL), lambda i: (i, 0)),
            pl.BlockSpec((1, N_ADA, D_MODEL), lambda i: (i // tiles_per_batch, 0, 0)),
            _resident((1, D_MODEL)),
            *w_specs,
            *cast_specs,
        ],
        out_specs=[pl.BlockSpec((FFN_ROW_TILE, D_MODEL), lambda i: (i, 0)), *cast_specs],
        out_shape=[jax.ShapeDtypeStruct(x2d.shape, x2d.dtype),
                   *[jax.ShapeDtypeStruct(arr.shape, jnp.bfloat16) for arr in cast]],
        scratch_shapes=scratch,
        compiler_params=pltpu.CompilerParams(
            dimension_semantics=("arbitrary",), vmem_limit_bytes=VMEM_LIMIT_BYTES),
        name="swiglu_sublayer",
    )(x2d, mod, g.reshape(1, D_MODEL), w_in, w_out, *cast)
    return out[0], [arr.reshape(shape) for arr, shape in zip(out[1:], cast_shapes)]


def _head_rms_norm(t, head_gain):
    lane = lax.broadcasted_iota(jnp.int32, (t.shape[0], LANES), 1)
    low = lane < HEAD_DIM
    pair_gain = jnp.concatenate([head_gain, head_gain], axis=1)
    parts = []
    for p in range(HEAD_PAIRS):
        blk = t[:, p * LANES:(p + 1) * LANES]
        sq = blk * blk
        r_low = lax.rsqrt(jnp.sum(jnp.where(low, sq, 0.0), axis=-1, keepdims=True) * (1.0 / HEAD_DIM) + EPS)
        r_high = lax.rsqrt(jnp.sum(jnp.where(low, 0.0, sq), axis=-1, keepdims=True) * (1.0 / HEAD_DIM) + EPS)
        scaled = blk * jnp.where(low, r_low, r_high) * pair_gain
        parts.append(scaled.astype(jnp.bfloat16))
    return jnp.concatenate(parts, axis=-1)


def _build_bias_table(diag_ref, bias_buf):
    r = lax.broadcasted_iota(jnp.int32, (Q_BLOCK, BIAS_PERIOD), 0)
    j = lax.broadcasted_iota(jnp.int32, (Q_BLOCK, BIAS_PERIOD), 1)
    rel_chunk = r // CHUNK + LEFT_CHUNKS - j // CHUNK
    in_band = jnp.logical_and(rel_chunk >= 0, rel_chunk <= LEFT_CHUNKS)
    for head in range(N_HEADS):
        diag = jnp.broadcast_to(diag_ref[head:head + 1, :], (Q_BLOCK, BIAS_PERIOD))
        table = pltpu.roll(diag, 0, 1, stride=1, stride_axis=0)
        table = jnp.where(in_band, table * LOG2_E, MASK_VALUE)
        row0 = (head % 2) * Q_BLOCK
        bias_buf[head // 2, row0:row0 + Q_BLOCK, :] = table[:, :KEY_WINDOW]
    bias_buf[HEAD_PAIRS:2 * HEAD_PAIRS] = jnp.full((HEAD_PAIRS, 2 * Q_BLOCK, KEY_WINDOW), MASK_VALUE,
                                                   bias_buf.dtype)


def _mix_kernel(x_ref, mod_ref, g_ref, win_ref, qg_ref, kg_ref, diag_ref, wa_ref, wg_ref,
                ps_ref, wp_ref, wo_ref, o_ref, k_buf, v_buf, u_buf, a_buf, bias_buf, wpool_buf):
    tile = pl.program_id(1)
    first = tile == 0

    @pl.when(jnp.logical_and(pl.program_id(0) == 0, first))
    def _():
        _build_bias_table(diag_ref, bias_buf)
        for gi in range(POOL_GROUPS):
            rows = slice(gi * POOL_GROUP_DIM, (gi + 1) * POOL_GROUP_DIM)
            scaled = (wg_ref[rows, :].astype(jnp.float32) * ps_ref[:, rows]).astype(jnp.bfloat16)
            wpool_buf[rows, :] = jnp.dot(scaled, wp_ref[rows, :],
                                         preferred_element_type=jnp.float32).astype(wpool_buf.dtype)
        for p in range(HEAD_PAIRS):
            v_buf[ROW_TILE:2 * ROW_TILE, p * V_GROUP + LANES:(p + 1) * V_GROUP] = jnp.ones(
                (ROW_TILE, LANES), v_buf.dtype)

    @pl.when(first)
    def _():
        k_buf[0:ROW_TILE, :] = jnp.zeros((ROW_TILE, ATTN_WIDTH), k_buf.dtype)
        v_buf[0:ROW_TILE, :] = jnp.zeros((ROW_TILE, HEAD_PAIRS * V_GROUP), v_buf.dtype)
        u_buf[0:POOL_HISTORY, :] = jnp.zeros((POOL_HISTORY, POOL_WIDTH), u_buf.dtype)

    x = x_ref[...]
    shift = mod_ref[0, 3:4, :]
    scale = mod_ref[0, 4:5, :]
    gate = mod_ref[0, 5:6, :]
    h = _norm_modulate(x, g_ref[...], shift, scale)

    def proj(col0, width):
        return jnp.dot(h, win_ref[:, col0:col0 + width], preferred_element_type=jnp.float32)

    u = proj(3 * ATTN_WIDTH, POOL_WIDTH)
    u_buf[POOL_HISTORY:POOL_HISTORY + ROW_TILE, :] = u
    pos = tile * ROW_TILE + lax.broadcasted_iota(jnp.int32, (ROW_TILE, 1), 0)
    sums = u_buf[...]
    pooled = []
    span = 1
    for gi, window in enumerate(POOL_WINDOWS):
        while span < window:
            sums = sums + pltpu.roll(sums, span, axis=0)
            span *= 2
        total = sums[POOL_HISTORY:, :POOL_GROUP_DIM]
        if gi + 1 < POOL_GROUPS:
            sums = sums[:, POOL_GROUP_DIM:]
        count = jnp.minimum(pos + 1, window).astype(jnp.float32)
        u_group = u[:, gi * POOL_GROUP_DIM:(gi + 1) * POOL_GROUP_DIM]
        pooled.append((total / count - u_group).astype(jnp.bfloat16))
    pooled = jnp.concatenate(pooled, axis=-1)
    u_buf[0:POOL_HISTORY, :] = u_buf[ROW_TILE:ROW_TILE + POOL_HISTORY, :]

    q = proj(0, ATTN_WIDTH)
    qn = _head_rms_norm(q, qg_ref[...] * (HEAD_DIM ** -0.5 * LOG2_E))
    k = proj(ATTN_WIDTH, ATTN_WIDTH)
    k_buf[ROW_TILE:2 * ROW_TILE, :] = _head_rms_norm(k, kg_ref[...])
    v = proj(2 * ATTN_WIDTH, ATTN_WIDTH)
    for p in range(HEAD_PAIRS):
        v_buf[ROW_TILE:2 * ROW_TILE, p * V_GROUP:p * V_GROUP + LANES] = (
            v[:, p * LANES:(p + 1) * LANES].astype(jnp.bfloat16))

    lane = lax.broadcasted_iota(jnp.int32, (Q_BLOCK, LANES), 1)
    low = lane < HEAD_DIM
    masked_table = first.astype(jnp.int32) * HEAD_PAIRS

    def scores(half, p):
        base = half * QK_ROWS
        cols = slice(p * LANES, (p + 1) * LANES)
        stack = []
        for blk in range(QK_ROWS // Q_BLOCK):
            q_pair = qn[base + blk * Q_BLOCK:base + (blk + 1) * Q_BLOCK, cols]
            zero = jnp.zeros_like(q_pair)
            stack += [jnp.where(low, q_pair, zero), jnp.where(low, zero, q_pair)]
        return lax.dot_general(jnp.concatenate(stack, axis=0), k_buf[base:base + QK_SPAN, cols],
                               (((1,), (1,)), ((), ())),
                               preferred_element_type=jnp.float32)

    def attend(half, p, s_all):
        cols = slice(p * LANES, (p + 1) * LANES)
        for blk in range(QK_ROWS // Q_BLOCK):
            row0 = half * QK_ROWS + blk * Q_BLOCK
            bias = jnp.concatenate(
                [bias_buf[(masked_table if c0 < ROW_TILE - row0 else 0) + p, :, c0:c0 + LANES]
                 for c0 in range(0, KEY_WINDOW, LANES)], axis=1)
            s = s_all[2 * blk * Q_BLOCK:2 * (blk + 1) * Q_BLOCK,
                      blk * Q_BLOCK:blk * Q_BLOCK + KEY_WINDOW] + bias
            m = jnp.max(s, axis=-1, keepdims=True)
            pexp = jnp.exp2(s - m).astype(jnp.bfloat16)
            o = jnp.dot(pexp, v_buf[row0:row0 + KEY_WINDOW, p * V_GROUP:(p + 1) * V_GROUP],
                        preferred_element_type=jnp.float32)
            o = o[:, :LANES] / o[:, LANES:]
            a_buf[row0:row0 + Q_BLOCK, cols] = jnp.where(
                low, o[:Q_BLOCK], o[Q_BLOCK:]).astype(a_buf.dtype)

    units = [(half, p) for half in range(ROW_TILE // QK_ROWS) for p in range(HEAD_PAIRS)]
    gate_col0 = 3 * ATTN_WIDTH + POOL_WIDTH
    gate_width = 2 * D_MODEL // len(units)
    gate_parts = [proj(gate_col0, gate_width)]
    s_next = scores(*units[0])
    for n, unit in enumerate(units):
        s_cur = s_next
        if n + 1 < len(units):
            s_next = scores(*units[n + 1])
            gate_parts.append(proj(gate_col0 + (n + 1) * gate_width, gate_width))
        attend(*unit, s_cur)
    gates = jnp.concatenate(gate_parts, axis=1)
    ya = jnp.dot(a_buf[...], wa_ref[...], preferred_element_type=jnp.float32)
    k_buf[0:ROW_TILE, :] = k_buf[ROW_TILE:2 * ROW_TILE, :]
    v_buf[0:ROW_TILE, :] = v_buf[ROW_TILE:2 * ROW_TILE, :]

    yb = jnp.dot(pooled, wpool_buf[...], preferred_element_type=jnp.float32)

    merged = (jax.nn.sigmoid(gates[:, :D_MODEL]) * ya
              + jax.nn.sigmoid(gates[:, D_MODEL:]) * yb).astype(jnp.bfloat16)
    o_ref[...] = x + gate * jnp.dot(merged, wo_ref[...], preferred_element_type=jnp.float32)


def _bias_diagonals(rel_bias):
    n_far = BAND - REL_CLIP + 1
    n_near = KEY_WINDOW - n_far
    far = rel_bias[:, 2 * REL_CLIP:]
    near = rel_bias[:, 2 * REL_CLIP - n_near:2 * REL_CLIP][:, ::-1]
    return jnp.concatenate(
        [jnp.broadcast_to(far, (N_HEADS, n_far)), near,
         jnp.broadcast_to(far, (N_HEADS, BIAS_PERIOD - KEY_WINDOW))], axis=1).astype(jnp.float32)


def _token_mix(x3d, mod, g, w_in, q_gain, k_gain, rel_bias, w_attn_out, w_pool_group, pool_scale,
               w_pool_out, w_o):
    n_batch, seq, _ = x3d.shape
    bf16 = jnp.bfloat16
    in_cols = w_in.shape[1]
    return pl.pallas_call(
        _mix_kernel,
        grid=(n_batch, seq // ROW_TILE),
        in_specs=[
            pl.BlockSpec((None, ROW_TILE, D_MODEL), lambda b, i: (b, i, 0)),
            pl.BlockSpec((1, N_ADA, D_MODEL), lambda b, i: (b, 0, 0)),
            _resident((1, D_MODEL)),
            _resident((D_MODEL, in_cols)),
            _resident((1, HEAD_DIM)),
            _resident((1, HEAD_DIM)),
            _resident((N_HEADS, BIAS_PERIOD)),
            _resident((ATTN_WIDTH, D_MODEL)),
            _resident((POOL_WIDTH, POOL_GROUP_DIM)),
            _resident((1, POOL_WIDTH)),
            _resident((POOL_WIDTH, D_MODEL)),
            _resident((D_MODEL, D_MODEL)),
        ],
        out_specs=pl.BlockSpec((None, ROW_TILE, D_MODEL), lambda b, i: (b, i, 0)),
        out_shape=jax.ShapeDtypeStruct(x3d.shape, x3d.dtype),
        scratch_shapes=[
            pltpu.VMEM((2 * ROW_TILE, ATTN_WIDTH), bf16),
            pltpu.VMEM((2 * ROW_TILE, HEAD_PAIRS * V_GROUP), bf16),
            pltpu.VMEM((POOL_HISTORY + ROW_TILE, POOL_WIDTH), jnp.float32),
            pltpu.VMEM((ROW_TILE, ATTN_WIDTH), bf16),
            pltpu.VMEM((2 * HEAD_PAIRS, 2 * Q_BLOCK, KEY_WINDOW), jnp.float32),
            pltpu.VMEM((POOL_WIDTH, D_MODEL), bf16),
        ],
        compiler_params=pltpu.CompilerParams(
            dimension_semantics=("arbitrary", "arbitrary"), vmem_limit_bytes=VMEM_LIMIT_BYTES),
        name="token_mix_sublayer",
    )(x3d, mod, g.reshape(1, D_MODEL), w_in,
      q_gain.reshape(1, HEAD_DIM), k_gain.reshape(1, HEAD_DIM),
      _bias_diagonals(rel_bias), w_attn_out, w_pool_group, pool_scale.reshape(1, POOL_WIDTH), w_pool_out, w_o)


def kernel(x, c, w_ada, b_ada, g_ffn1, w_ffn1_in, w_ffn1_out, g_mix, w_in, q_gain, k_gain, rel_bias,
           w_attn_out, w_pool_group, pool_scale, w_pool_out, w_o, g_ffn2, w_ffn2_in, w_ffn2_out):
    n_batch, seq, d_model = x.shape
    depth = w_ada.shape[0]
    assert d_model == D_MODEL and seq % ROW_TILE == 0 and seq % FFN_ROW_TILE == 0
    tiles_per_batch = seq // FFN_ROW_TILE
    for l in range(depth):
        mod, _ = _modulation(c, w_ada[l], b_ada[l])
        x2d = x.reshape(n_batch * seq, D_MODEL)
        later = (w_in[l], w_attn_out[l], w_pool_group[l].reshape(POOL_WIDTH, POOL_GROUP_DIM),
                 w_pool_out[l], w_o[l], w_ffn2_in[l], w_ffn2_out[l])
        x2d, later = _ffn(x2d, mod, g_ffn1[l], w_ffn1_in[l], w_ffn1_out[l], mod_row=0,
                          tiles_per_batch=tiles_per_batch, cast=later)
        w_in_b, w_attn_out_b, w_pool_group_b, w_pool_out_b, w_o_b, w_ffn2_in_b, w_ffn2_out_b = later
        x = _token_mix(x2d.reshape(n_batch, seq, D_MODEL), mod, g_mix[l], w_in_b, q_gain[l], k_gain[l],
                       rel_bias[l], w_attn_out_b, w_pool_group_b, pool_scale[l], w_pool_out_b, w_o_b)
        x2d, _ = _ffn(x.reshape(n_batch * seq, D_MODEL), mod, g_ffn2[l], w_ffn2_in_b, w_ffn2_out_b,
                      mod_row=6, tiles_per_batch=tiles_per_batch)
        x = x2d.reshape(n_batch, seq, D_MODEL)
    return x
```

```python
import functools
import math

import jax
import jax.numpy as jnp
from jax import lax
from jax.experimental import pallas as pl
from jax.experimental.pallas import tpu as pltpu

D_MODEL = 1024
CHUNK = 64
LEFT_CHUNKS = 8
N_HEADS = 8
HEAD_DIM = 64
ATTN_WIDTH = N_HEADS * HEAD_DIM
POOL_WINDOWS = (2, 4, 8, 16)
POOL_GROUPS = len(POOL_WINDOWS)
POOL_WIDTH = 512
POOL_GROUP_DIM = POOL_WIDTH // POOL_GROUPS
REL_CLIP = 128
D_FF = 2816
N_ADA = 9
EPS = 1e-6
MASK_VALUE = -1e30
LOG2_E = math.log2(math.e)

LANES = 128
MXU_DIM = 256
BF16_SUBLANES = 16
VMEM_LIMIT_BYTES = 56 * 1024 * 1024

ROW_TILE = 512
FFN_ROW_TILE = 1024
FFN_PREP_BLOCKS = 4
FF_CHUNK = MXU_DIM
N_FF_CHUNKS = D_FF // FF_CHUNK
BAND = LEFT_CHUNKS * CHUNK
Q_BLOCK = 2 * CHUNK
KEY_WINDOW = Q_BLOCK + BAND
QK_ROWS = 2 * Q_BLOCK
QK_SPAN = QK_ROWS + BAND
HEAD_PAIRS = ATTN_WIDTH // LANES
V_GROUP = 2 * LANES
BIAS_PERIOD = 768
POOL_HISTORY = 16
MOD_STEPS = 8

assert D_FF % FF_CHUNK == 0
assert ROW_TILE == BAND
assert ROW_TILE % QK_ROWS == 0 and 2 * HEAD_DIM == LANES
assert BIAS_PERIOD >= Q_BLOCK + KEY_WINDOW - 1 and BIAS_PERIOD % LANES == 0
assert POOL_HISTORY >= max(POOL_WINDOWS) - 1 and POOL_HISTORY % 8 == 0
assert all(w & (w - 1) == 0 for w in POOL_WINDOWS) and list(POOL_WINDOWS) == sorted(POOL_WINDOWS)
assert D_MODEL % (MOD_STEPS * 8) == 0


def _resident(shape):
    zeros = (0,) * len(shape)
    return pl.BlockSpec(shape, lambda *_: zeros, pipeline_mode=pl.Buffered(1))


def _row_blockable(arr, n_blocks):
    rows, cols = arr.shape
    split = 1
    while (rows * split) % (n_blocks * BF16_SUBLANES) or (cols // split) % LANES:
        split *= 2
        assert cols % split == 0, (arr.shape, n_blocks)
    return arr.reshape(rows * split, cols // split)


def _cast_specs(cast, n_blocks):
    views = [_row_blockable(arr, n_blocks) for arr in cast]
    specs = [pl.BlockSpec((v.shape[0] // n_blocks, v.shape[1]), lambda i: (i, 0)) for v in views]
    return views, specs


def _mod_kernel(ct_ref, w_ref, b_ref, *rest, n_cast):
    cast_src, o_ref, cast_dst = rest[:n_cast], rest[n_cast], rest[n_cast + 1:]

    n_batch, n_ada, width = o_ref.shape

    @pl.when(pl.program_id(0) == 0)
    def _():
        for j in range(n_ada):
            o_ref[:, j, :] = jnp.broadcast_to(b_ref[:, j * width:(j + 1) * width], (n_batch, width))

    for b in range(n_batch):
        col = ct_ref[:, b:b + 1]
        col = col * jax.nn.sigmoid(col)
        total = jnp.sum(w_ref[...] * col, axis=0, keepdims=True)
        for j in range(n_ada):
            o_ref[b, j:j + 1, :] += total[:, j * width:(j + 1) * width]
    for src, dst in zip(cast_src, cast_dst):
        dst[...] = src[...].astype(dst.dtype)


def _modulation(c, w_ada, b_ada, cast=()):
    n_batch = c.shape[0]
    n_cols = w_ada.shape[1]
    rows = D_MODEL // MOD_STEPS
    cast_shapes = [arr.shape for arr in cast]
    cast, cast_specs = _cast_specs(cast, MOD_STEPS)
    out = pl.pallas_call(
        functools.partial(_mod_kernel, n_cast=len(cast)),
        grid=(MOD_STEPS,),
        in_specs=[
            pl.BlockSpec((rows, n_batch), lambda j: (j, 0)),
            pl.BlockSpec((rows, n_cols), lambda j: (j, 0)),
            pl.BlockSpec((1, n_cols), lambda j: (0, 0)),
            *cast_specs,
        ],
        out_specs=[pl.BlockSpec((n_batch, N_ADA, D_MODEL), lambda j: (0, 0, 0)), *cast_specs],
        out_shape=[jax.ShapeDtypeStruct((n_batch, N_ADA, D_MODEL), jnp.float32),
                   *[jax.ShapeDtypeStruct(arr.shape, jnp.bfloat16) for arr in cast]],
        compiler_params=pltpu.CompilerParams(
            dimension_semantics=("arbitrary",), vmem_limit_bytes=VMEM_LIMIT_BYTES),
        name="adaln_modulation",
    )(c.T, w_ada, b_ada.reshape(1, n_cols), *cast)
    return out[0], [arr.reshape(shape) for arr, shape in zip(out[1:], cast_shapes)]


def _norm_modulate(x, g, shift, scale):
    ms = jnp.mean(x * x, axis=-1, keepdims=True)
    col_gain = g * (1.0 + scale)
    return (x * lax.rsqrt(ms + EPS) * col_gain + shift).astype(jnp.bfloat16)


def _ffn_kernel(*refs, mod_row, n_cast, lookahead):
    if lookahead:
        zero_ref, x_ref, xn_ref, mod_ref, modn_ref, g_ref, win_ref, wout_ref, *rest = refs
    else:
        x_ref, mod_ref, g_ref, win_ref, wout_ref, *rest = refs
    cast_src, o_ref, cast_dst = rest[:n_cast], rest[n_cast], rest[n_cast + 1:2 * n_cast + 1]

    def normed(rows_ref, m_ref, r0, r1):
        return _norm_modulate(rows_ref[r0:r1, :], g_ref[...], m_ref[0, mod_row:mod_row + 1, :],
                              m_ref[0, mod_row + 1:mod_row + 2, :])

    if lookahead:
        h_buf, hn_buf = rest[2 * n_cast + 1:]

        @pl.when(pl.program_id(0) == 0)
        def _():
            h_buf[...] = normed(x_ref, mod_ref, 0, FFN_ROW_TILE)

        def h():
            return h_buf[...]
    else:
        h_now = normed(x_ref, mod_ref, 0, FFN_ROW_TILE)

        def h():
            return h_now

    prep_rows = FFN_ROW_TILE // FFN_PREP_BLOCKS
    anchor = None
    acc = None
    for c in range(N_FF_CHUNKS):
        lo = c * FF_CHUNK
        a = jnp.dot(h(), win_ref[:, lo:lo + FF_CHUNK], preferred_element_type=jnp.float32)
        b = jnp.dot(h(), win_ref[:, D_FF + lo:D_FF + lo + FF_CHUNK], preferred_element_type=jnp.float32)
        if anchor is not None:
            top = jnp.concatenate([a[:8, :LANES] + anchor, a[:8, LANES:]], axis=1)
            a = jnp.concatenate([top, a[8:]], axis=0)
            anchor = None
        if lookahead and c < FFN_PREP_BLOCKS:
            blk = normed(xn_ref, modn_ref, c * prep_rows, (c + 1) * prep_rows)
            hn_buf[c * prep_rows:(c + 1) * prep_rows, :] = blk
            bits = pltpu.bitcast(blk, jnp.int32)
            folded = bits[0:8, :LANES]
            for r0 in range(0, bits.shape[0], 8):
                for c0 in range(0, bits.shape[1], LANES):
                    if r0 or c0:
                        folded = folded | bits[r0:r0 + 8, c0:c0 + LANES]
            anchor = pltpu.bitcast(folded & zero_ref[0], jnp.float32)
        if lookahead and c == N_FF_CHUNKS - 1:
            h_buf[...] = hn_buf[...]
        act = (a * jax.nn.sigmoid(a) * b).astype(jnp.bfloat16)
        part = jnp.dot(act, wout_ref[lo:lo + FF_CHUNK, :], preferred_element_type=jnp.float32)
        acc = part if acc is None else acc + part
    gate = mod_ref[0, mod_row + 2:mod_row + 3, :]
    o_ref[...] = x_ref[...] + (0.5 * gate) * acc
    for src, dst in zip(cast_src, cast_dst):
        dst[...] = src[...].astype(dst.dtype)


def _ffn(x2d, mod, g, w_in, w_out, *, mod_row, tiles_per_batch, cast=(), lookahead=False):
    n_rows = x2d.shape[0]
    n_steps = n_rows // FFN_ROW_TILE
    cast_shapes = [arr.shape for arr in cast]
    cast, cast_specs = _cast_specs(cast, n_steps)

    def following(i):
        return jnp.minimum(i + 1, n_steps - 1)

    x_spec = pl.BlockSpec((FFN_ROW_TILE, D_MODEL), lambda i: (i, 0))
    mod_spec = pl.BlockSpec((1, N_ADA, D_MODEL), lambda i: (i // tiles_per_batch, 0, 0))
    if lookahead:
        row_specs = [pl.BlockSpec(memory_space=pltpu.SMEM), x_spec,
                     pl.BlockSpec((FFN_ROW_TILE, D_MODEL), lambda i: (following(i), 0)), mod_spec,
                     pl.BlockSpec((1, N_ADA, D_MODEL), lambda i: (following(i) // tiles_per_batch, 0, 0))]
        row_args = (jnp.zeros((1,), jnp.int32), x2d, x2d, mod, mod)
        scratch = [pltpu.VMEM((FFN_ROW_TILE, D_MODEL), jnp.bfloat16),
                   pltpu.VMEM((FFN_ROW_TILE, D_MODEL), jnp.bfloat16)]
    else:
        row_specs, row_args, scratch = [x_spec, mod_spec], (x2d, mod), []
    out = pl.pallas_call(
        functools.partial(_ffn_kernel, mod_row=mod_row, n_cast=len(cast), lookahead=lookahead),
        grid=(n_steps,),
        in_specs=[
            *row_specs,
            _resident((1, D_MODEL)),
            _resident((D_MODEL, 2 * D_FF)),
            _resident((D_FF, D_MODEL)),
            *cast_specs,
        ],
        out_specs=[x_spec, *cast_specs],
        out_shape=[jax.ShapeDtypeStruct(x2d.shape, x2d.dtype),
                   *[jax.ShapeDtypeStruct(arr.shape, jnp.bfloat16) for arr in cast]],
        scratch_shapes=scratch,
        compiler_params=pltpu.CompilerParams(
            dimension_semantics=("arbitrary",), vmem_limit_bytes=VMEM_LIMIT_BYTES),
        name="swiglu_sublayer",
    )(*row_args, g.reshape(1, D_MODEL), w_in, w_out, *cast)
    return out[0], [arr.reshape(shape) for arr, shape in zip(out[1:], cast_shapes)]


def _head_rms_norm(t, head_gain):
    lane = lax.broadcasted_iota(jnp.int32, (t.shape[0], LANES), 1)
    low = lane < HEAD_DIM
    pair_gain = jnp.concatenate([head_gain, head_gain], axis=1)
    parts = []
    for p in range(HEAD_PAIRS):
        blk = t[:, p * LANES:(p + 1) * LANES]
        sq = blk * blk
        r_low = lax.rsqrt(jnp.sum(jnp.where(low, sq, 0.0), axis=-1, keepdims=True) * (1.0 / HEAD_DIM) + EPS)
        r_high = lax.rsqrt(jnp.sum(jnp.where(low, 0.0, sq), axis=-1, keepdims=True) * (1.0 / HEAD_DIM) + EPS)
        scaled = blk * jnp.where(low, r_low, r_high) * pair_gain
        parts.append(scaled.astype(jnp.bfloat16))
    return jnp.concatenate(parts, axis=-1)


def _build_bias_table(diag_ref, bias_buf):
    r = lax.broadcasted_iota(jnp.int32, (Q_BLOCK, BIAS_PERIOD), 0)
    j = lax.broadcasted_iota(jnp.int32, (Q_BLOCK, BIAS_PERIOD), 1)
    rel_chunk = r // CHUNK + LEFT_CHUNKS - j // CHUNK
    in_band = jnp.logical_and(rel_chunk >= 0, rel_chunk <= LEFT_CHUNKS)
    for head in range(N_HEADS):
        diag = jnp.broadcast_to(diag_ref[head:head + 1, :], (Q_BLOCK, BIAS_PERIOD))
        table = pltpu.roll(diag, 0, 1, stride=1, stride_axis=0)
        table = jnp.where(in_band, table * LOG2_E, MASK_VALUE)
        row0 = (head % 2) * Q_BLOCK
        bias_buf[head // 2, row0:row0 + Q_BLOCK, :] = table[:, :KEY_WINDOW]
    bias_buf[HEAD_PAIRS:2 * HEAD_PAIRS] = jnp.full((HEAD_PAIRS, 2 * Q_BLOCK, KEY_WINDOW), MASK_VALUE,
                                                   bias_buf.dtype)


def _mix_kernel(x_ref, mod_ref, g_ref, win_ref, qg_ref, kg_ref, diag_ref, wa_ref, wg_ref,
                ps_ref, wp_ref, wo_ref, o_ref, k_buf, v_buf, u_buf, a_buf, bias_buf, wpool_buf):
    tile = pl.program_id(1)
    first = tile == 0

    @pl.when(jnp.logical_and(pl.program_id(0) == 0, first))
    def _():
        _build_bias_table(diag_ref, bias_buf)
        for gi in range(POOL_GROUPS):
            rows = slice(gi * POOL_GROUP_DIM, (gi + 1) * POOL_GROUP_DIM)
            scaled = (wg_ref[rows, :].astype(jnp.float32) * ps_ref[:, rows]).astype(jnp.bfloat16)
            wpool_buf[rows, :] = jnp.dot(scaled, wp_ref[rows, :],
                                         preferred_element_type=jnp.float32).astype(wpool_buf.dtype)
        for p in range(HEAD_PAIRS):
            v_buf[ROW_TILE:2 * ROW_TILE, p * V_GROUP + LANES:(p + 1) * V_GROUP] = jnp.ones(
                (ROW_TILE, LANES), v_buf.dtype)

    @pl.when(first)
    def _():
        k_buf[0:ROW_TILE, :] = jnp.zeros((ROW_TILE, ATTN_WIDTH), k_buf.dtype)
        v_buf[0:ROW_TILE, :] = jnp.zeros((ROW_TILE, HEAD_PAIRS * V_GROUP), v_buf.dtype)
        u_buf[0:POOL_HISTORY, :] = jnp.zeros((POOL_HISTORY, POOL_WIDTH), u_buf.dtype)

    x = x_ref[...]
    shift = mod_ref[0, 3:4, :]
    scale = mod_ref[0, 4:5, :]
    gate = mod_ref[0, 5:6, :]
    h = _norm_modulate(x, g_ref[...], shift, scale)

    def proj(col0, width):
        return jnp.dot(h, win_ref[:, col0:col0 + width], preferred_element_type=jnp.float32)

    u = proj(3 * ATTN_WIDTH, POOL_WIDTH)
    u_buf[POOL_HISTORY:POOL_HISTORY + ROW_TILE, :] = u
    pos = tile * ROW_TILE + lax.broadcasted_iota(jnp.int32, (ROW_TILE, 1), 0)
    sums = u_buf[...]
    pooled = []
    span = 1
    for gi, window in enumerate(POOL_WINDOWS):
        while span < window:
            sums = sums + pltpu.roll(sums, span, axis=0)
            span *= 2
        total = sums[POOL_HISTORY:, :POOL_GROUP_DIM]
        if gi + 1 < POOL_GROUPS:
            sums = sums[:, POOL_GROUP_DIM:]
        count = jnp.minimum(pos + 1, window).astype(jnp.float32)
        u_group = u[:, gi * POOL_GROUP_DIM:(gi + 1) * POOL_GROUP_DIM]
        pooled.append((total / count - u_group).astype(jnp.bfloat16))
    pooled = jnp.concatenate(pooled, axis=-1)
    u_buf[0:POOL_HISTORY, :] = u_buf[ROW_TILE:ROW_TILE + POOL_HISTORY, :]

    q = proj(0, ATTN_WIDTH)
    qn = _head_rms_norm(q, qg_ref[...] * (HEAD_DIM ** -0.5 * LOG2_E))
    k = proj(ATTN_WIDTH, ATTN_WIDTH)
    k_buf[ROW_TILE:2 * ROW_TILE, :] = _head_rms_norm(k, kg_ref[...])
    v = proj(2 * ATTN_WIDTH, ATTN_WIDTH)
    for p in range(HEAD_PAIRS):
        v_buf[ROW_TILE:2 * ROW_TILE, p * V_GROUP:p * V_GROUP + LANES] = (
            v[:, p * LANES:(p + 1) * LANES].astype(jnp.bfloat16))

    lane = lax.broadcasted_iota(jnp.int32, (Q_BLOCK, LANES), 1)
    low = lane < HEAD_DIM
    masked_table = first.astype(jnp.int32) * HEAD_PAIRS

    def scores(half, p):
        base = half * QK_ROWS
        cols = slice(p * LANES, (p + 1) * LANES)
        stack = []
        for blk in range(QK_ROWS // Q_BLOCK):
            q_pair = qn[base + blk * Q_BLOCK:base + (blk + 1) * Q_BLOCK, cols]
            zero = jnp.zeros_like(q_pair)
            stack += [jnp.where(low, q_pair, zero), jnp.where(low, zero, q_pair)]
        return lax.dot_general(jnp.concatenate(stack, axis=0), k_buf[base:base + QK_SPAN, cols],
                               (((1,), (1,)), ((), ())),
                               preferred_element_type=jnp.float32)

    def attend(half, p, s_all):
        cols = slice(p * LANES, (p + 1) * LANES)
        for blk in range(QK_ROWS // Q_BLOCK):
            row0 = half * QK_ROWS + blk * Q_BLOCK
            bias = jnp.concatenate(
                [bias_buf[(masked_table if c0 < ROW_TILE - row0 else 0) + p, :, c0:c0 + LANES]
                 for c0 in range(0, KEY_WINDOW, LANES)], axis=1)
            s = s_all[2 * blk * Q_BLOCK:2 * (blk + 1) * Q_BLOCK,
                      blk * Q_BLOCK:blk * Q_BLOCK + KEY_WINDOW] + bias
            m = jnp.max(s, axis=-1, keepdims=True)
            pexp = jnp.exp2(s - m).astype(jnp.bfloat16)
            o = jnp.dot(pexp, v_buf[row0:row0 + KEY_WINDOW, p * V_GROUP:(p + 1) * V_GROUP],
                        preferred_element_type=jnp.float32)
            o = o[:, :LANES] / o[:, LANES:]
            a_buf[row0:row0 + Q_BLOCK, cols] = jnp.where(
                low, o[:Q_BLOCK], o[Q_BLOCK:]).astype(a_buf.dtype)

    units = [(half, p) for half in range(ROW_TILE // QK_ROWS) for p in range(HEAD_PAIRS)]
    gate_col0 = 3 * ATTN_WIDTH + POOL_WIDTH
    gate_width = 2 * D_MODEL // len(units)
    gate_parts = [proj(gate_col0, gate_width)]
    s_next = scores(*units[0])
    for n, unit in enumerate(units):
        s_cur = s_next
        if n + 1 < len(units):
            s_next = scores(*units[n + 1])
            gate_parts.append(proj(gate_col0 + (n + 1) * gate_width, gate_width))
        attend(*unit, s_cur)
    gates = jnp.concatenate(gate_parts, axis=1)
    ya = jnp.dot(a_buf[...], wa_ref[...], preferred_element_type=jnp.float32)
    k_buf[0:ROW_TILE, :] = k_buf[ROW_TILE:2 * ROW_TILE, :]
    v_buf[0:ROW_TILE, :] = v_buf[ROW_TILE:2 * ROW_TILE, :]

    yb = jnp.dot(pooled, wpool_buf[...], preferred_element_type=jnp.float32)

    merged = (jax.nn.sigmoid(gates[:, :D_MODEL]) * ya
              + jax.nn.sigmoid(gates[:, D_MODEL:]) * yb).astype(jnp.bfloat16)
    o_ref[...] = x + gate * jnp.dot(merged, wo_ref[...], preferred_element_type=jnp.float32)


def _bias_diagonals(rel_bias):
    n_far = BAND - REL_CLIP + 1
    n_near = KEY_WINDOW - n_far
    far = rel_bias[:, 2 * REL_CLIP:]
    near = rel_bias[:, 2 * REL_CLIP - n_near:2 * REL_CLIP][:, ::-1]
    return jnp.concatenate(
        [jnp.broadcast_to(far, (N_HEADS, n_far)), near,
         jnp.broadcast_to(far, (N_HEADS, BIAS_PERIOD - KEY_WINDOW))], axis=1).astype(jnp.float32)


def _token_mix(x3d, mod, g, w_in, q_gain, k_gain, rel_bias, w_attn_out, w_pool_group, pool_scale,
               w_pool_out, w_o):
    n_batch, seq, _ = x3d.shape
    bf16 = jnp.bfloat16
    in_cols = w_in.shape[1]
    return pl.pallas_call(
        _mix_kernel,
        grid=(n_batch, seq // ROW_TILE),
        in_specs=[
            pl.BlockSpec((None, ROW_TILE, D_MODEL), lambda b, i: (b, i, 0)),
            pl.BlockSpec((1, N_ADA, D_MODEL), lambda b, i: (b, 0, 0)),
            _resident((1, D_MODEL)),
            _resident((D_MODEL, in_cols)),
            _resident((1, HEAD_DIM)),
            _resident((1, HEAD_DIM)),
            _resident((N_HEADS, BIAS_PERIOD)),
            _resident((ATTN_WIDTH, D_MODEL)),
            _resident((POOL_WIDTH, POOL_GROUP_DIM)),
            _resident((1, POOL_WIDTH)),
            _resident((POOL_WIDTH, D_MODEL)),
            _resident((D_MODEL, D_MODEL)),
        ],
        out_specs=pl.BlockSpec((None, ROW_TILE, D_MODEL), lambda b, i: (b, i, 0)),
        out_shape=jax.ShapeDtypeStruct(x3d.shape, x3d.dtype),
        scratch_shapes=[
            pltpu.VMEM((2 * ROW_TILE, ATTN_WIDTH), bf16),
            pltpu.VMEM((2 * ROW_TILE, HEAD_PAIRS * V_GROUP), bf16),
            pltpu.VMEM((POOL_HISTORY + ROW_TILE, POOL_WIDTH), jnp.float32),
            pltpu.VMEM((ROW_TILE, ATTN_WIDTH), bf16),
            pltpu.VMEM((2 * HEAD_PAIRS, 2 * Q_BLOCK, KEY_WINDOW), jnp.float32),
            pltpu.VMEM((POOL_WIDTH, D_MODEL), bf16),
        ],
        compiler_params=pltpu.CompilerParams(
            dimension_semantics=("arbitrary", "arbitrary"), vmem_limit_bytes=VMEM_LIMIT_BYTES),
        name="token_mix_sublayer",
    )(x3d, mod, g.reshape(1, D_MODEL), w_in,
      q_gain.reshape(1, HEAD_DIM), k_gain.reshape(1, HEAD_DIM),
      _bias_diagonals(rel_bias), w_attn_out, w_pool_group, pool_scale.reshape(1, POOL_WIDTH), w_pool_out, w_o)


def kernel(x, c, w_ada, b_ada, g_ffn1, w_ffn1_in, w_ffn1_out, g_mix, w_in, q_gain, k_gain, rel_bias,
           w_attn_out, w_pool_group, pool_scale, w_pool_out, w_o, g_ffn2, w_ffn2_in, w_ffn2_out):
    n_batch, seq, d_model = x.shape
    depth = w_ada.shape[0]
    assert d_model == D_MODEL and seq % ROW_TILE == 0 and seq % FFN_ROW_TILE == 0
    tiles_per_batch = seq // FFN_ROW_TILE
    for l in range(depth):
        mod, (w_ffn1_in_b, w_ffn1_out_b) = _modulation(c, w_ada[l], b_ada[l],
                                                        cast=(w_ffn1_in[l], w_ffn1_out[l]))
        x2d = x.reshape(n_batch * seq, D_MODEL)
        later = (w_in[l], w_attn_out[l], w_pool_group[l].reshape(POOL_WIDTH, POOL_GROUP_DIM),
                 w_pool_out[l], w_o[l], w_ffn2_in[l], w_ffn2_out[l])
        x2d, later = _ffn(x2d, mod, g_ffn1[l], w_ffn1_in_b, w_ffn1_out_b, mod_row=0,
                          tiles_per_batch=tiles_per_batch, cast=later)
        w_in_b, w_attn_out_b, w_pool_group_b, w_pool_out_b, w_o_b, w_ffn2_in_b, w_ffn2_out_b = later
        x = _token_mix(x2d.reshape(n_batch, seq, D_MODEL), mod, g_mix[l], w_in_b, q_gain[l], k_gain[l],
                       rel_bias[l], w_attn_out_b, w_pool_group_b, pool_scale[l], w_pool_out_b, w_o_b)
        x2d, _ = _ffn(x.reshape(n_batch * seq, D_MODEL), mod, g_ffn2[l], w_ffn2_in_b, w_ffn2_out_b,
                      mod_row=6, tiles_per_batch=tiles_per_batch, lookahead=True)
        x = x2d.reshape(n_batch, seq, D_MODEL)
    return x
```

```python
import functools
import math

import jax
import jax.numpy as jnp
from jax import lax
from jax.experimental import pallas as pl
from jax.experimental.pallas import tpu as pltpu

D_MODEL = 1024
CHUNK = 64
LEFT_CHUNKS = 8
N_HEADS = 8
HEAD_DIM = 64
ATTN_WIDTH = N_HEADS * HEAD_DIM
POOL_WINDOWS = (2, 4, 8, 16)
POOL_GROUPS = len(POOL_WINDOWS)
POOL_WIDTH = 512
POOL_GROUP_DIM = POOL_WIDTH // POOL_GROUPS
REL_CLIP = 128
D_FF = 2816
N_ADA = 9
EPS = 1e-6
MASK_VALUE = -1e30
LOG2_E = math.log2(math.e)

LANES = 128
MXU_DIM = 256
F32_SUBLANES = 8
BF16_SUBLANES = 16
VMEM_LIMIT_BYTES = 56 * 1024 * 1024

ROW_TILE = 512
FFN_ROW_TILE = 1024
FF_CHUNK = MXU_DIM
N_FF_CHUNKS = D_FF // FF_CHUNK
BAND = LEFT_CHUNKS * CHUNK
Q_BLOCK = 2 * CHUNK
KEY_WINDOW = Q_BLOCK + BAND
QK_ROWS = 2 * Q_BLOCK
QK_SPAN = QK_ROWS + BAND
HEAD_PAIRS = ATTN_WIDTH // LANES
V_GROUP = 2 * LANES
BIAS_PERIOD = 768
POOL_HISTORY = 16
MOD_STEPS = 8

assert D_FF % FF_CHUNK == 0
assert ROW_TILE == BAND
assert ROW_TILE % QK_ROWS == 0 and 2 * HEAD_DIM == LANES
assert BIAS_PERIOD >= Q_BLOCK + KEY_WINDOW - 1 and BIAS_PERIOD % LANES == 0
assert POOL_HISTORY >= max(POOL_WINDOWS) - 1 and POOL_HISTORY % F32_SUBLANES == 0
assert all(w & (w - 1) == 0 for w in POOL_WINDOWS) and list(POOL_WINDOWS) == sorted(POOL_WINDOWS)
assert D_MODEL % (MOD_STEPS * F32_SUBLANES) == 0


def _resident(shape):
    zeros = (0,) * len(shape)
    return pl.BlockSpec(shape, lambda *_: zeros, pipeline_mode=pl.Buffered(1))


def _row_blockable(arr, n_blocks):
    rows, cols = arr.shape
    split = 1
    while (rows * split) % (n_blocks * BF16_SUBLANES) or (cols // split) % LANES:
        split *= 2
        assert cols % split == 0, (arr.shape, n_blocks)
    return arr.reshape(rows * split, cols // split)


def _cast_specs(cast, n_blocks):
    views = [_row_blockable(arr, n_blocks) for arr in cast]
    specs = [pl.BlockSpec((v.shape[0] // n_blocks, v.shape[1]), lambda i: (i, 0)) for v in views]
    return views, specs


def _mod_kernel(ct_ref, w_ref, b_ref, *rest, n_cast):
    cast_src, o_ref, cast_dst = rest[:n_cast], rest[n_cast], rest[n_cast + 1:]

    n_batch, n_ada, width = o_ref.shape

    @pl.when(pl.program_id(0) == 0)
    def _():
        for j in range(n_ada):
            o_ref[:, j, :] = jnp.broadcast_to(b_ref[:, j * width:(j + 1) * width], (n_batch, width))

    for b in range(n_batch):
        col = ct_ref[:, b:b + 1]
        col = col * jax.nn.sigmoid(col)
        total = jnp.sum(w_ref[...] * col, axis=0, keepdims=True)
        for j in range(n_ada):
            o_ref[b, j:j + 1, :] += total[:, j * width:(j + 1) * width]
    for src, dst in zip(cast_src, cast_dst):
        dst[...] = src[...].astype(dst.dtype)


def _modulation(c, w_ada, b_ada, cast=()):
    n_batch = c.shape[0]
    n_cols = w_ada.shape[1]
    rows = D_MODEL // MOD_STEPS
    cast_shapes = [arr.shape for arr in cast]
    cast, cast_specs = _cast_specs(cast, MOD_STEPS)
    out = pl.pallas_call(
        functools.partial(_mod_kernel, n_cast=len(cast)),
        grid=(MOD_STEPS,),
        in_specs=[
            pl.BlockSpec((rows, n_batch), lambda j: (j, 0)),
            pl.BlockSpec((rows, n_cols), lambda j: (j, 0)),
            pl.BlockSpec((1, n_cols), lambda j: (0, 0)),
            *cast_specs,
        ],
        out_specs=[pl.BlockSpec((n_batch, N_ADA, D_MODEL), lambda j: (0, 0, 0)), *cast_specs],
        out_shape=[jax.ShapeDtypeStruct((n_batch, N_ADA, D_MODEL), jnp.float32),
                   *[jax.ShapeDtypeStruct(arr.shape, jnp.bfloat16) for arr in cast]],
        compiler_params=pltpu.CompilerParams(
            dimension_semantics=("arbitrary",), vmem_limit_bytes=VMEM_LIMIT_BYTES),
        name="adaln_modulation",
    )(c.T, w_ada, b_ada.reshape(1, n_cols), *cast)
    return out[0], [arr.reshape(shape) for arr, shape in zip(out[1:], cast_shapes)]


def _norm_modulate(x, g, shift, scale):
    ms = jnp.mean(x * x, axis=-1, keepdims=True)
    col_gain = g * (1.0 + scale)
    return (x * lax.rsqrt(ms + EPS) * col_gain + shift).astype(jnp.bfloat16)


def _ffn_kernel(x_ref, mod_ref, g_ref, win_ref, wout_ref, *rest, mod_row, n_cast):
    cast_src, o_ref, cast_dst = rest[:n_cast], rest[n_cast], rest[n_cast + 1:]
    x = x_ref[...]
    shift = mod_ref[0, mod_row:mod_row + 1, :]
    scale = mod_ref[0, mod_row + 1:mod_row + 2, :]
    gate = mod_ref[0, mod_row + 2:mod_row + 3, :]
    h = _norm_modulate(x, g_ref[...], shift, scale)
    acc = None
    for c in range(N_FF_CHUNKS):
        lo = c * FF_CHUNK
        a = jnp.dot(h, win_ref[:, lo:lo + FF_CHUNK], preferred_element_type=jnp.float32)
        b = jnp.dot(h, win_ref[:, D_FF + lo:D_FF + lo + FF_CHUNK], preferred_element_type=jnp.float32)
        act = (a * jax.nn.sigmoid(a) * b).astype(jnp.bfloat16)
        part = jnp.dot(act, wout_ref[lo:lo + FF_CHUNK, :], preferred_element_type=jnp.float32)
        acc = part if acc is None else acc + part
    o_ref[...] = x + (0.5 * gate) * acc
    for src, dst in zip(cast_src, cast_dst):
        dst[...] = src[...].astype(dst.dtype)


def _ffn(x2d, mod, g, w_in, w_out, *, mod_row, tiles_per_batch, cast=()):
    n_rows = x2d.shape[0]
    n_steps = n_rows // FFN_ROW_TILE
    cast_shapes = [arr.shape for arr in cast]
    cast, cast_specs = _cast_specs(cast, n_steps)
    out = pl.pallas_call(
        functools.partial(_ffn_kernel, mod_row=mod_row, n_cast=len(cast)),
        grid=(n_steps,),
        in_specs=[
            pl.BlockSpec((FFN_ROW_TILE, D_MODEL), lambda i: (i, 0)),
            pl.BlockSpec((1, N_ADA, D_MODEL), lambda i: (i // tiles_per_batch, 0, 0)),
            _resident((1, D_MODEL)),
            _resident((D_MODEL, 2 * D_FF)),
            _resident((D_FF, D_MODEL)),
            *cast_specs,
        ],
        out_specs=[pl.BlockSpec((FFN_ROW_TILE, D_MODEL), lambda i: (i, 0)), *cast_specs],
        out_shape=[jax.ShapeDtypeStruct(x2d.shape, x2d.dtype),
                   *[jax.ShapeDtypeStruct(arr.shape, jnp.bfloat16) for arr in cast]],
        compiler_params=pltpu.CompilerParams(
            dimension_semantics=("arbitrary",), vmem_limit_bytes=VMEM_LIMIT_BYTES),
        name="swiglu_sublayer",
    )(x2d, mod, g.reshape(1, D_MODEL), w_in, w_out, *cast)
    return out[0], [arr.reshape(shape) for arr, shape in zip(out[1:], cast_shapes)]


def _head_rms_norm(t, head_gain):
    lane = lax.broadcasted_iota(jnp.int32, (t.shape[0], LANES), 1)
    low = lane < HEAD_DIM
    pair_gain = jnp.concatenate([head_gain, head_gain], axis=1)
    parts = []
    for p in range(HEAD_PAIRS):
        blk = t[:, p * LANES:(p + 1) * LANES]
        sq = blk * blk
        r_low = lax.rsqrt(jnp.sum(jnp.where(low, sq, 0.0), axis=-1, keepdims=True) * (1.0 / HEAD_DIM) + EPS)
        r_high = lax.rsqrt(jnp.sum(jnp.where(low, 0.0, sq), axis=-1, keepdims=True) * (1.0 / HEAD_DIM) + EPS)
        scaled = blk * jnp.where(low, r_low, r_high) * pair_gain
        parts.append(scaled.astype(jnp.bfloat16))
    return jnp.concatenate(parts, axis=-1)


def _build_bias_table(diag_ref, bias_buf):
    r = lax.broadcasted_iota(jnp.int32, (Q_BLOCK, BIAS_PERIOD), 0)
    j = lax.broadcasted_iota(jnp.int32, (Q_BLOCK, BIAS_PERIOD), 1)
    rel_chunk = r // CHUNK + LEFT_CHUNKS - j // CHUNK
    in_band = jnp.logical_and(rel_chunk >= 0, rel_chunk <= LEFT_CHUNKS)
    for head in range(N_HEADS):
        diag = jnp.broadcast_to(diag_ref[head:head + 1, :], (Q_BLOCK, BIAS_PERIOD))
        table = pltpu.roll(diag, 0, 1, stride=1, stride_axis=0)
        table = jnp.where(in_band, table * LOG2_E, MASK_VALUE)
        row0 = (head % 2) * Q_BLOCK
        bias_buf[head // 2, row0:row0 + Q_BLOCK, :] = table[:, :KEY_WINDOW]
    bias_buf[HEAD_PAIRS:2 * HEAD_PAIRS] = jnp.full((HEAD_PAIRS, 2 * Q_BLOCK, KEY_WINDOW), MASK_VALUE,
                                                   bias_buf.dtype)


def _mix_kernel(x_ref, mod_ref, g_ref, win_ref, qg_ref, kg_ref, diag_ref, wa_ref, wg_ref,
                ps_ref, wp_ref, wo_ref, o_ref, k_buf, v_buf, u_buf, a_buf, bias_buf, wpool_buf):
    tile = pl.program_id(1)
    first = tile == 0

    @pl.when(jnp.logical_and(pl.program_id(0) == 0, first))
    def _():
        _build_bias_table(diag_ref, bias_buf)
        for gi in range(POOL_GROUPS):
            rows = slice(gi * POOL_GROUP_DIM, (gi + 1) * POOL_GROUP_DIM)
            scaled = (wg_ref[rows, :].astype(jnp.float32) * ps_ref[:, rows]).astype(jnp.bfloat16)
            wpool_buf[rows, :] = jnp.dot(scaled, wp_ref[rows, :],
                                         preferred_element_type=jnp.float32).astype(wpool_buf.dtype)
        for p in range(HEAD_PAIRS):
            v_buf[ROW_TILE:2 * ROW_TILE, p * V_GROUP + LANES:(p + 1) * V_GROUP] = jnp.ones(
                (ROW_TILE, LANES), v_buf.dtype)

    @pl.when(first)
    def _():
        k_buf[0:ROW_TILE, :] = jnp.zeros((ROW_TILE, ATTN_WIDTH), k_buf.dtype)
        v_buf[0:ROW_TILE, :] = jnp.zeros((ROW_TILE, HEAD_PAIRS * V_GROUP), v_buf.dtype)
        u_buf[0:POOL_HISTORY, :] = jnp.zeros((POOL_HISTORY, POOL_WIDTH), u_buf.dtype)

    x = x_ref[...]
    shift = mod_ref[0, 3:4, :]
    scale = mod_ref[0, 4:5, :]
    gate = mod_ref[0, 5:6, :]
    h = _norm_modulate(x, g_ref[...], shift, scale)

    def proj(col0, width):
        return jnp.dot(h, win_ref[:, col0:col0 + width], preferred_element_type=jnp.float32)

    u = proj(3 * ATTN_WIDTH, POOL_WIDTH)
    u_buf[POOL_HISTORY:POOL_HISTORY + ROW_TILE, :] = u
    pos = tile * ROW_TILE + lax.broadcasted_iota(jnp.int32, (ROW_TILE, 1), 0)
    sums = u_buf[...]
    pooled = []
    span = 1
    for gi, window in enumerate(POOL_WINDOWS):
        while span < window:
            sums = sums + pltpu.roll(sums, span, axis=0)
            span *= 2
        total = sums[POOL_HISTORY:, :POOL_GROUP_DIM]
        if gi + 1 < POOL_GROUPS:
            sums = sums[:, POOL_GROUP_DIM:]
        count = jnp.minimum(pos + 1, window).astype(jnp.float32)
        u_group = u[:, gi * POOL_GROUP_DIM:(gi + 1) * POOL_GROUP_DIM]
        pooled.append((total / count - u_group).astype(jnp.bfloat16))
    pooled = jnp.concatenate(pooled, axis=-1)
    u_buf[0:POOL_HISTORY, :] = u_buf[ROW_TILE:ROW_TILE + POOL_HISTORY, :]

    q = proj(0, ATTN_WIDTH)
    qn = _head_rms_norm(q, qg_ref[...] * (HEAD_DIM ** -0.5 * LOG2_E))
    k = proj(ATTN_WIDTH, ATTN_WIDTH)
    k_buf[ROW_TILE:2 * ROW_TILE, :] = _head_rms_norm(k, kg_ref[...])
    v = proj(2 * ATTN_WIDTH, ATTN_WIDTH)
    for p in range(HEAD_PAIRS):
        v_buf[ROW_TILE:2 * ROW_TILE, p * V_GROUP:p * V_GROUP + LANES] = (
            v[:, p * LANES:(p + 1) * LANES].astype(jnp.bfloat16))

    lane = lax.broadcasted_iota(jnp.int32, (Q_BLOCK, LANES), 1)
    low = lane < HEAD_DIM
    masked_table = first.astype(jnp.int32) * HEAD_PAIRS

    def scores(half, p):
        base = half * QK_ROWS
        cols = slice(p * LANES, (p + 1) * LANES)
        stack = []
        for blk in range(QK_ROWS // Q_BLOCK):
            q_pair = qn[base + blk * Q_BLOCK:base + (blk + 1) * Q_BLOCK, cols]
            zero = jnp.zeros_like(q_pair)
            stack += [jnp.where(low, q_pair, zero), jnp.where(low, zero, q_pair)]
        return lax.dot_general(jnp.concatenate(stack, axis=0), k_buf[base:base + QK_SPAN, cols],
                               (((1,), (1,)), ((), ())),
                               preferred_element_type=jnp.float32)

    def attend(half, p, s_all):
        cols = slice(p * LANES, (p + 1) * LANES)
        for blk in range(QK_ROWS // Q_BLOCK):
            row0 = half * QK_ROWS + blk * Q_BLOCK
            bias = jnp.concatenate(
                [bias_buf[(masked_table if c0 < ROW_TILE - row0 else 0) + p, :, c0:c0 + LANES]
                 for c0 in range(0, KEY_WINDOW, LANES)], axis=1)
            s = s_all[2 * blk * Q_BLOCK:2 * (blk + 1) * Q_BLOCK,
                      blk * Q_BLOCK:blk * Q_BLOCK + KEY_WINDOW] + bias
            m = jnp.max(s, axis=-1, keepdims=True)
            pexp = jnp.exp2(s - m).astype(jnp.bfloat16)
            o = jnp.dot(pexp, v_buf[row0:row0 + KEY_WINDOW, p * V_GROUP:(p + 1) * V_GROUP],
                        preferred_element_type=jnp.float32)
            o = o[:, :LANES] / o[:, LANES:]
            a_buf[row0:row0 + Q_BLOCK, cols] = jnp.where(
                low, o[:Q_BLOCK], o[Q_BLOCK:]).astype(a_buf.dtype)

    units = [(half, p) for half in range(ROW_TILE // QK_ROWS) for p in range(HEAD_PAIRS)]
    gate_col0 = 3 * ATTN_WIDTH + POOL_WIDTH
    gate_width = 2 * D_MODEL // len(units)
    gate_parts = [proj(gate_col0, gate_width)]
    s_next = scores(*units[0])
    for n, unit in enumerate(units):
        s_cur = s_next
        if n + 1 < len(units):
            s_next = scores(*units[n + 1])
            gate_parts.append(proj(gate_col0 + (n + 1) * gate_width, gate_width))
        attend(*unit, s_cur)
    gates = jnp.concatenate(gate_parts, axis=1)
    ya = jnp.dot(a_buf[...], wa_ref[...], preferred_element_type=jnp.float32)
    k_buf[0:ROW_TILE, :] = k_buf[ROW_TILE:2 * ROW_TILE, :]
    v_buf[0:ROW_TILE, :] = v_buf[ROW_TILE:2 * ROW_TILE, :]

    yb = jnp.dot(pooled, wpool_buf[...], preferred_element_type=jnp.float32)

    merged = (jax.nn.sigmoid(gates[:, :D_MODEL]) * ya
              + jax.nn.sigmoid(gates[:, D_MODEL:]) * yb).astype(jnp.bfloat16)
    o_ref[...] = x + gate * jnp.dot(merged, wo_ref[...], preferred_element_type=jnp.float32)


def _bias_diagonals(rel_bias):
    n_far = BAND - REL_CLIP + 1
    n_near = KEY_WINDOW - n_far
    far = rel_bias[:, 2 * REL_CLIP:]
    near = rel_bias[:, 2 * REL_CLIP - n_near:2 * REL_CLIP][:, ::-1]
    return jnp.concatenate(
        [jnp.broadcast_to(far, (N_HEADS, n_far)), near,
         jnp.broadcast_to(far, (N_HEADS, BIAS_PERIOD - KEY_WINDOW))], axis=1).astype(jnp.float32)


def _token_mix(x3d, mod, g, w_in, q_gain, k_gain, rel_bias, w_attn_out, w_pool_group, pool_scale,
               w_pool_out, w_o):
    n_batch, seq, _ = x3d.shape
    bf16 = jnp.bfloat16
    in_cols = w_in.shape[1]
    return pl.pallas_call(
        _mix_kernel,
        grid=(n_batch, seq // ROW_TILE),
        in_specs=[
            pl.BlockSpec((None, ROW_TILE, D_MODEL), lambda b, i: (b, i, 0)),
            pl.BlockSpec((1, N_ADA, D_MODEL), lambda b, i: (b, 0, 0)),
            _resident((1, D_MODEL)),
            _resident((D_MODEL, in_cols)),
            _resident((1, HEAD_DIM)),
            _resident((1, HEAD_DIM)),
            _resident((N_HEADS, BIAS_PERIOD)),
            _resident((ATTN_WIDTH, D_MODEL)),
            _resident((POOL_WIDTH, POOL_GROUP_DIM)),
            _resident((1, POOL_WIDTH)),
            _resident((POOL_WIDTH, D_MODEL)),
            _resident((D_MODEL, D_MODEL)),
        ],
        out_specs=pl.BlockSpec((None, ROW_TILE, D_MODEL), lambda b, i: (b, i, 0)),
        out_shape=jax.ShapeDtypeStruct(x3d.shape, x3d.dtype),
        scratch_shapes=[
            pltpu.VMEM((2 * ROW_TILE, ATTN_WIDTH), bf16),
            pltpu.VMEM((2 * ROW_TILE, HEAD_PAIRS * V_GROUP), bf16),
            pltpu.VMEM((POOL_HISTORY + ROW_TILE, POOL_WIDTH), jnp.float32),
            pltpu.VMEM((ROW_TILE, ATTN_WIDTH), bf16),
            pltpu.VMEM((2 * HEAD_PAIRS, 2 * Q_BLOCK, KEY_WINDOW), jnp.float32),
            pltpu.VMEM((POOL_WIDTH, D_MODEL), bf16),
        ],
        compiler_params=pltpu.CompilerParams(
            dimension_semantics=("arbitrary", "arbitrary"), vmem_limit_bytes=VMEM_LIMIT_BYTES),
        name="token_mix_sublayer",
    )(x3d, mod, g.reshape(1, D_MODEL), w_in,
      q_gain.reshape(1, HEAD_DIM), k_gain.reshape(1, HEAD_DIM),
      _bias_diagonals(rel_bias), w_attn_out, w_pool_group, pool_scale.reshape(1, POOL_WIDTH), w_pool_out, w_o)


def kernel(x, c, w_ada, b_ada, g_ffn1, w_ffn1_in, w_ffn1_out, g_mix, w_in, q_gain, k_gain, rel_bias,
           w_attn_out, w_pool_group, pool_scale, w_pool_out, w_o, g_ffn2, w_ffn2_in, w_ffn2_out):
    n_batch, seq, d_model = x.shape
    depth = w_ada.shape[0]
    assert d_model == D_MODEL and seq % ROW_TILE == 0 and seq % FFN_ROW_TILE == 0
    tiles_per_batch = seq // FFN_ROW_TILE
    for l in range(depth):
        mod, (w_ffn1_in_b, w_ffn1_out_b) = _modulation(c, w_ada[l], b_ada[l],
                                                        cast=(w_ffn1_in[l], w_ffn1_out[l]))
        x2d = x.reshape(n_batch * seq, D_MODEL)
        later = (w_in[l], w_attn_out[l], w_pool_group[l].reshape(POOL_WIDTH, POOL_GROUP_DIM),
                 w_pool_out[l], w_o[l], w_ffn2_in[l], w_ffn2_out[l])
        x2d, later = _ffn(x2d, mod, g_ffn1[l], w_ffn1_in_b, w_ffn1_out_b, mod_row=0,
                          tiles_per_batch=tiles_per_batch, cast=later)
        w_in_b, w_attn_out_b, w_pool_group_b, w_pool_out_b, w_o_b, w_ffn2_in_b, w_ffn2_out_b = later
        x = _token_mix(x2d.reshape(n_batch, seq, D_MODEL), mod, g_mix[l], w_in_b, q_gain[l], k_gain[l],
                       rel_bias[l], w_attn_out_b, w_pool_group_b, pool_scale[l], w_pool_out_b, w_o_b)
        x2d, _ = _ffn(x.reshape(n_batch * seq, D_MODEL), mod, g_ffn2[l], w_ffn2_in_b, w_ffn2_out_b,
                      mod_row=6, tiles_per_batch=tiles_per_batch)
        x = x2d.reshape(n_batch, seq, D_MODEL)
    return x
```

```python
import functools
import math

import jax
import jax.numpy as jnp
from jax import lax
from jax.experimental import pallas as pl
from jax.experimental.pallas import tpu as pltpu

D_MODEL = 1024
CHUNK = 64
LEFT_CHUNKS = 8
N_HEADS = 8
HEAD_DIM = 64
ATTN_WIDTH = N_HEADS * HEAD_DIM
POOL_WINDOWS = (2, 4, 8, 16)
POOL_GROUPS = len(POOL_WINDOWS)
POOL_WIDTH = 512
POOL_GROUP_DIM = POOL_WIDTH // POOL_GROUPS
REL_CLIP = 128
D_FF = 2816
N_ADA = 9
EPS = 1e-6
MASK_VALUE = -1e30
LOG2_E = math.log2(math.e)

LANES = 128
MXU_DIM = 256
F32_SUBLANES = 8
BF16_SUBLANES = 16
VMEM_LIMIT_BYTES = 56 * 1024 * 1024

ROW_TILE = 512
FFN_ROW_TILE = 1024
FF_CHUNK = MXU_DIM
N_FF_CHUNKS = D_FF // FF_CHUNK
BAND = LEFT_CHUNKS * CHUNK
Q_BLOCK = 2 * CHUNK
KEY_WINDOW = Q_BLOCK + BAND
QK_ROWS = 2 * Q_BLOCK
QK_SPAN = QK_ROWS + BAND
HEAD_PAIRS = ATTN_WIDTH // LANES
V_GROUP = 2 * LANES
BIAS_PERIOD = 768
POOL_HISTORY = 16
MOD_STEPS = 8

assert D_FF % FF_CHUNK == 0
assert ROW_TILE == BAND
assert ROW_TILE % QK_ROWS == 0 and 2 * HEAD_DIM == LANES
assert BIAS_PERIOD >= Q_BLOCK + KEY_WINDOW - 1 and BIAS_PERIOD % LANES == 0
assert POOL_HISTORY >= max(POOL_WINDOWS) - 1 and POOL_HISTORY % F32_SUBLANES == 0
assert all(w & (w - 1) == 0 for w in POOL_WINDOWS) and list(POOL_WINDOWS) == sorted(POOL_WINDOWS)
assert D_MODEL % (MOD_STEPS * F32_SUBLANES) == 0


def _resident(shape):
    zeros = (0,) * len(shape)
    return pl.BlockSpec(shape, lambda *_: zeros, pipeline_mode=pl.Buffered(1))


def _row_blockable(arr, n_blocks):
    rows, cols = arr.shape
    split = 1
    while (rows * split) % (n_blocks * BF16_SUBLANES) or (cols // split) % LANES:
        split *= 2
        assert cols % split == 0, (arr.shape, n_blocks)
    return arr.reshape(rows * split, cols // split)


def _cast_specs(cast, n_blocks):
    views = [_row_blockable(arr, n_blocks) for arr in cast]
    specs = [pl.BlockSpec((v.shape[0] // n_blocks, v.shape[1]), lambda i: (i, 0)) for v in views]
    return views, specs


def _mod_kernel(ct_ref, w_ref, b_ref, *rest, n_cast):
    cast_src, o_ref, cast_dst = rest[:n_cast], rest[n_cast], rest[n_cast + 1:]

    n_batch, n_ada, width = o_ref.shape

    @pl.when(pl.program_id(0) == 0)
    def _():
        for j in range(n_ada):
            o_ref[:, j, :] = jnp.broadcast_to(b_ref[:, j * width:(j + 1) * width], (n_batch, width))

    for b in range(n_batch):
        col = ct_ref[:, b:b + 1]
        col = col * jax.nn.sigmoid(col)
        total = jnp.sum(w_ref[...] * col, axis=0, keepdims=True)
        for j in range(n_ada):
            o_ref[b, j:j + 1, :] += total[:, j * width:(j + 1) * width]
    for src, dst in zip(cast_src, cast_dst):
        dst[...] = src[...].astype(dst.dtype)


def _modulation(c, w_ada, b_ada, cast=()):
    n_batch = c.shape[0]
    n_cols = w_ada.shape[1]
    rows = D_MODEL // MOD_STEPS
    cast_shapes = [arr.shape for arr in cast]
    cast, cast_specs = _cast_specs(cast, MOD_STEPS)
    out = pl.pallas_call(
        functools.partial(_mod_kernel, n_cast=len(cast)),
        grid=(MOD_STEPS,),
        in_specs=[
            pl.BlockSpec((rows, n_batch), lambda j: (j, 0)),
            pl.BlockSpec((rows, n_cols), lambda j: (j, 0)),
            pl.BlockSpec((1, n_cols), lambda j: (0, 0)),
            *cast_specs,
        ],
        out_specs=[pl.BlockSpec((n_batch, N_ADA, D_MODEL), lambda j: (0, 0, 0)), *cast_specs],
        out_shape=[jax.ShapeDtypeStruct((n_batch, N_ADA, D_MODEL), jnp.float32),
                   *[jax.ShapeDtypeStruct(arr.shape, jnp.bfloat16) for arr in cast]],
        compiler_params=pltpu.CompilerParams(
            dimension_semantics=("arbitrary",), vmem_limit_bytes=VMEM_LIMIT_BYTES),
        name="adaln_modulation",
    )(c.T, w_ada, b_ada.reshape(1, n_cols), *cast)
    return out[0], [arr.reshape(shape) for arr, shape in zip(out[1:], cast_shapes)]


def _norm_modulate(x, g, shift, scale):
    ms = jnp.mean(x * x, axis=-1, keepdims=True)
    col_gain = g * (1.0 + scale)
    return (x * lax.rsqrt(ms + EPS) * col_gain + shift).astype(jnp.bfloat16)


def _ffn_kernel(x_ref, mod_ref, g_ref, win_ref, wout_ref, *rest, mod_row, n_cast):
    cast_src, o_ref, cast_dst = rest[:n_cast], rest[n_cast], rest[n_cast + 1:]
    x = x_ref[...]
    shift = mod_ref[0, mod_row:mod_row + 1, :]
    scale = mod_ref[0, mod_row + 1:mod_row + 2, :]
    gate = mod_ref[0, mod_row + 2:mod_row + 3, :]
    h = _norm_modulate(x, g_ref[...], shift, scale)
    acts = []
    for c in range(N_FF_CHUNKS):
        lo = c * FF_CHUNK
        a = jnp.dot(h, win_ref[:, lo:lo + FF_CHUNK], preferred_element_type=jnp.float32)
        b = jnp.dot(h, win_ref[:, D_FF + lo:D_FF + lo + FF_CHUNK], preferred_element_type=jnp.float32)
        acts.append((a * jax.nn.sigmoid(a) * b).astype(jnp.bfloat16))
    acc = jnp.dot(jnp.concatenate(acts, axis=1), wout_ref[...], preferred_element_type=jnp.float32)
    o_ref[...] = x + (0.5 * gate) * acc
    for src, dst in zip(cast_src, cast_dst):
        dst[...] = src[...].astype(dst.dtype)


def _ffn(x2d, mod, g, w_in, w_out, *, mod_row, tiles_per_batch, cast=()):
    n_rows = x2d.shape[0]
    n_steps = n_rows // FFN_ROW_TILE
    cast_shapes = [arr.shape for arr in cast]
    cast, cast_specs = _cast_specs(cast, n_steps)
    out = pl.pallas_call(
        functools.partial(_ffn_kernel, mod_row=mod_row, n_cast=len(cast)),
        grid=(n_steps,),
        in_specs=[
            pl.BlockSpec((FFN_ROW_TILE, D_MODEL), lambda i: (i, 0)),
            pl.BlockSpec((1, N_ADA, D_MODEL), lambda i: (i // tiles_per_batch, 0, 0)),
            _resident((1, D_MODEL)),
            _resident((D_MODEL, 2 * D_FF)),
            _resident((D_FF, D_MODEL)),
            *cast_specs,
        ],
        out_specs=[pl.BlockSpec((FFN_ROW_TILE, D_MODEL), lambda i: (i, 0)), *cast_specs],
        out_shape=[jax.ShapeDtypeStruct(x2d.shape, x2d.dtype),
                   *[jax.ShapeDtypeStruct(arr.shape, jnp.bfloat16) for arr in cast]],
        compiler_params=pltpu.CompilerParams(
            dimension_semantics=("arbitrary",), vmem_limit_bytes=VMEM_LIMIT_BYTES),
        name="swiglu_sublayer",
    )(x2d, mod, g.reshape(1, D_MODEL), w_in, w_out, *cast)
    return out[0], [arr.reshape(shape) for arr, shape in zip(out[1:], cast_shapes)]


def _head_rms_norm(t, head_gain):
    lane = lax.broadcasted_iota(jnp.int32, (t.shape[0], LANES), 1)
    low = lane < HEAD_DIM
    pair_gain = jnp.concatenate([head_gain, head_gain], axis=1)
    parts = []
    for p in range(HEAD_PAIRS):
        blk = t[:, p * LANES:(p + 1) * LANES]
        sq = blk * blk
        r_low = lax.rsqrt(jnp.sum(jnp.where(low, sq, 0.0), axis=-1, keepdims=True) * (1.0 / HEAD_DIM) + EPS)
        r_high = lax.rsqrt(jnp.sum(jnp.where(low, 0.0, sq), axis=-1, keepdims=True) * (1.0 / HEAD_DIM) + EPS)
        scaled = blk * jnp.where(low, r_low, r_high) * pair_gain
        parts.append(scaled.astype(jnp.bfloat16))
    return jnp.concatenate(parts, axis=-1)


def _build_bias_table(diag_ref, bias_buf):
    r = lax.broadcasted_iota(jnp.int32, (Q_BLOCK, BIAS_PERIOD), 0)
    j = lax.broadcasted_iota(jnp.int32, (Q_BLOCK, BIAS_PERIOD), 1)
    rel_chunk = r // CHUNK + LEFT_CHUNKS - j // CHUNK
    in_band = jnp.logical_and(rel_chunk >= 0, rel_chunk <= LEFT_CHUNKS)
    for head in range(N_HEADS):
        diag = jnp.broadcast_to(diag_ref[head:head + 1, :], (Q_BLOCK, BIAS_PERIOD))
        table = pltpu.roll(diag, 0, 1, stride=1, stride_axis=0)
        table = jnp.where(in_band, table * LOG2_E, MASK_VALUE)
        row0 = (head % 2) * Q_BLOCK
        bias_buf[head // 2, row0:row0 + Q_BLOCK, :] = table[:, :KEY_WINDOW]
    bias_buf[HEAD_PAIRS:2 * HEAD_PAIRS] = jnp.full((HEAD_PAIRS, 2 * Q_BLOCK, KEY_WINDOW), MASK_VALUE,
                                                   bias_buf.dtype)


def _mix_kernel(x_ref, mod_ref, g_ref, win_ref, qg_ref, kg_ref, diag_ref, wa_ref, wg_ref,
                ps_ref, wp_ref, wo_ref, o_ref, k_buf, v_buf, u_buf, a_buf, bias_buf, wpool_buf):
    tile = pl.program_id(1)
    first = tile == 0

    @pl.when(jnp.logical_and(pl.program_id(0) == 0, first))
    def _():
        _build_bias_table(diag_ref, bias_buf)
        for gi in range(POOL_GROUPS):
            rows = slice(gi * POOL_GROUP_DIM, (gi + 1) * POOL_GROUP_DIM)
            scaled = (wg_ref[rows, :].astype(jnp.float32) * ps_ref[:, rows]).astype(jnp.bfloat16)
            wpool_buf[rows, :] = jnp.dot(scaled, wp_ref[rows, :],
                                         preferred_element_type=jnp.float32).astype(wpool_buf.dtype)
        for p in range(HEAD_PAIRS):
            v_buf[ROW_TILE:2 * ROW_TILE, p * V_GROUP + LANES:(p + 1) * V_GROUP] = jnp.ones(
                (ROW_TILE, LANES), v_buf.dtype)

    @pl.when(first)
    def _():
        k_buf[0:ROW_TILE, :] = jnp.zeros((ROW_TILE, ATTN_WIDTH), k_buf.dtype)
        v_buf[0:ROW_TILE, :] = jnp.zeros((ROW_TILE, HEAD_PAIRS * V_GROUP), v_buf.dtype)
        u_buf[0:POOL_HISTORY, :] = jnp.zeros((POOL_HISTORY, POOL_WIDTH), u_buf.dtype)

    x = x_ref[...]
    shift = mod_ref[0, 3:4, :]
    scale = mod_ref[0, 4:5, :]
    gate = mod_ref[0, 5:6, :]
    h = _norm_modulate(x, g_ref[...], shift, scale)

    def proj(col0, width):
        return jnp.dot(h, win_ref[:, col0:col0 + width], preferred_element_type=jnp.float32)

    u = proj(3 * ATTN_WIDTH, POOL_WIDTH)
    u_buf[POOL_HISTORY:POOL_HISTORY + ROW_TILE, :] = u
    pos = tile * ROW_TILE + lax.broadcasted_iota(jnp.int32, (ROW_TILE, 1), 0)
    sums = u_buf[...]
    pooled = []
    span = 1
    for gi, window in enumerate(POOL_WINDOWS):
        while span < window:
            sums = sums + pltpu.roll(sums, span, axis=0)
            span *= 2
        total = sums[POOL_HISTORY:, :POOL_GROUP_DIM]
        if gi + 1 < POOL_GROUPS:
            sums = sums[:, POOL_GROUP_DIM:]
        count = jnp.minimum(pos + 1, window).astype(jnp.float32)
        u_group = u[:, gi * POOL_GROUP_DIM:(gi + 1) * POOL_GROUP_DIM]
        pooled.append((total / count - u_group).astype(jnp.bfloat16))
    pooled = jnp.concatenate(pooled, axis=-1)
    u_buf[0:POOL_HISTORY, :] = u_buf[ROW_TILE:ROW_TILE + POOL_HISTORY, :]

    q = proj(0, ATTN_WIDTH)
    qn = _head_rms_norm(q, qg_ref[...] * (HEAD_DIM ** -0.5 * LOG2_E))
    k = proj(ATTN_WIDTH, ATTN_WIDTH)
    k_buf[ROW_TILE:2 * ROW_TILE, :] = _head_rms_norm(k, kg_ref[...])
    v = proj(2 * ATTN_WIDTH, ATTN_WIDTH)
    for p in range(HEAD_PAIRS):
        v_buf[ROW_TILE:2 * ROW_TILE, p * V_GROUP:p * V_GROUP + LANES] = (
            v[:, p * LANES:(p + 1) * LANES].astype(jnp.bfloat16))

    lane = lax.broadcasted_iota(jnp.int32, (Q_BLOCK, LANES), 1)
    low = lane < HEAD_DIM
    masked_table = first.astype(jnp.int32) * HEAD_PAIRS

    def scores(half, p):
        base = half * QK_ROWS
        cols = slice(p * LANES, (p + 1) * LANES)
        stack = []
        for blk in range(QK_ROWS // Q_BLOCK):
            q_pair = qn[base + blk * Q_BLOCK:base + (blk + 1) * Q_BLOCK, cols]
            zero = jnp.zeros_like(q_pair)
            stack += [jnp.where(low, q_pair, zero), jnp.where(low, zero, q_pair)]
        return lax.dot_general(jnp.concatenate(stack, axis=0), k_buf[base:base + QK_SPAN, cols],
                               (((1,), (1,)), ((), ())),
                               preferred_element_type=jnp.float32)

    def attend(half, p, s_all):
        cols = slice(p * LANES, (p + 1) * LANES)
        for blk in range(QK_ROWS // Q_BLOCK):
            row0 = half * QK_ROWS + blk * Q_BLOCK
            bias = jnp.concatenate(
                [bias_buf[(masked_table if c0 < ROW_TILE - row0 else 0) + p, :, c0:c0 + LANES]
                 for c0 in range(0, KEY_WINDOW, LANES)], axis=1)
            s = s_all[2 * blk * Q_BLOCK:2 * (blk + 1) * Q_BLOCK,
                      blk * Q_BLOCK:blk * Q_BLOCK + KEY_WINDOW] + bias
            m = jnp.max(s, axis=-1, keepdims=True)
            pexp = jnp.exp2(s - m).astype(jnp.bfloat16)
            o = jnp.dot(pexp, v_buf[row0:row0 + KEY_WINDOW, p * V_GROUP:(p + 1) * V_GROUP],
                        preferred_element_type=jnp.float32)
            o = o[:, :LANES] / o[:, LANES:]
            a_buf[row0:row0 + Q_BLOCK, cols] = jnp.where(
                low, o[:Q_BLOCK], o[Q_BLOCK:]).astype(a_buf.dtype)

    units = [(half, p) for half in range(ROW_TILE // QK_ROWS) for p in range(HEAD_PAIRS)]
    gate_col0 = 3 * ATTN_WIDTH + POOL_WIDTH
    gate_width = 2 * D_MODEL // len(units)
    gate_parts = [proj(gate_col0, gate_width)]
    s_next = scores(*units[0])
    for n, unit in enumerate(units):
        s_cur = s_next
        if n + 1 < len(units):
            s_next = scores(*units[n + 1])
            gate_parts.append(proj(gate_col0 + (n + 1) * gate_width, gate_width))
        attend(*unit, s_cur)
    gates = jnp.concatenate(gate_parts, axis=1)
    ya = jnp.dot(a_buf[...], wa_ref[...], preferred_element_type=jnp.float32)
    k_buf[0:ROW_TILE, :] = k_buf[ROW_TILE:2 * ROW_TILE, :]
    v_buf[0:ROW_TILE, :] = v_buf[ROW_TILE:2 * ROW_TILE, :]

    yb = jnp.dot(pooled, wpool_buf[...], preferred_element_type=jnp.float32)

    merged = (jax.nn.sigmoid(gates[:, :D_MODEL]) * ya
              + jax.nn.sigmoid(gates[:, D_MODEL:]) * yb).astype(jnp.bfloat16)
    o_ref[...] = x + gate * jnp.dot(merged, wo_ref[...], preferred_element_type=jnp.float32)


def _bias_diagonals(rel_bias):
    n_far = BAND - REL_CLIP + 1
    n_near = KEY_WINDOW - n_far
    far = rel_bias[:, 2 * REL_CLIP:]
    near = rel_bias[:, 2 * REL_CLIP - n_near:2 * REL_CLIP][:, ::-1]
    return jnp.concatenate(
        [jnp.broadcast_to(far, (N_HEADS, n_far)), near,
         jnp.broadcast_to(far, (N_HEADS, BIAS_PERIOD - KEY_WINDOW))], axis=1).astype(jnp.float32)


def _token_mix(x3d, mod, g, w_in, q_gain, k_gain, rel_bias, w_attn_out, w_pool_group, pool_scale,
               w_pool_out, w_o):
    n_batch, seq, _ = x3d.shape
    bf16 = jnp.bfloat16
    in_cols = w_in.shape[1]
    return pl.pallas_call(
        _mix_kernel,
        grid=(n_batch, seq // ROW_TILE),
        in_specs=[
            pl.BlockSpec((None, ROW_TILE, D_MODEL), lambda b, i: (b, i, 0)),
            pl.BlockSpec((1, N_ADA, D_MODEL), lambda b, i: (b, 0, 0)),
            _resident((1, D_MODEL)),
            _resident((D_MODEL, in_cols)),
            _resident((1, HEAD_DIM)),
            _resident((1, HEAD_DIM)),
            _resident((N_HEADS, BIAS_PERIOD)),
            _resident((ATTN_WIDTH, D_MODEL)),
            _resident((POOL_WIDTH, POOL_GROUP_DIM)),
            _resident((1, POOL_WIDTH)),
            _resident((POOL_WIDTH, D_MODEL)),
            _resident((D_MODEL, D_MODEL)),
        ],
        out_specs=pl.BlockSpec((None, ROW_TILE, D_MODEL), lambda b, i: (b, i, 0)),
        out_shape=jax.ShapeDtypeStruct(x3d.shape, x3d.dtype),
        scratch_shapes=[
            pltpu.VMEM((2 * ROW_TILE, ATTN_WIDTH), bf16),
            pltpu.VMEM((2 * ROW_TILE, HEAD_PAIRS * V_GROUP), bf16),
            pltpu.VMEM((POOL_HISTORY + ROW_TILE, POOL_WIDTH), jnp.float32),
            pltpu.VMEM((ROW_TILE, ATTN_WIDTH), bf16),
            pltpu.VMEM((2 * HEAD_PAIRS, 2 * Q_BLOCK, KEY_WINDOW), jnp.float32),
            pltpu.VMEM((POOL_WIDTH, D_MODEL), bf16),
        ],
        compiler_params=pltpu.CompilerParams(
            dimension_semantics=("arbitrary", "arbitrary"), vmem_limit_bytes=VMEM_LIMIT_BYTES),
        name="token_mix_sublayer",
    )(x3d, mod, g.reshape(1, D_MODEL), w_in,
      q_gain.reshape(1, HEAD_DIM), k_gain.reshape(1, HEAD_DIM),
      _bias_diagonals(rel_bias), w_attn_out, w_pool_group, pool_scale.reshape(1, POOL_WIDTH), w_pool_out, w_o)


def kernel(x, c, w_ada, b_ada, g_ffn1, w_ffn1_in, w_ffn1_out, g_mix, w_in, q_gain, k_gain, rel_bias,
           w_attn_out, w_pool_group, pool_scale, w_pool_out, w_o, g_ffn2, w_ffn2_in, w_ffn2_out):
    n_batch, seq, d_model = x.shape
    depth = w_ada.shape[0]
    assert d_model == D_MODEL and seq % ROW_TILE == 0 and seq % FFN_ROW_TILE == 0
    tiles_per_batch = seq // FFN_ROW_TILE
    for l in range(depth):
        mod, (w_ffn1_in_b, w_ffn1_out_b) = _modulation(c, w_ada[l], b_ada[l],
                                                        cast=(w_ffn1_in[l], w_ffn1_out[l]))
        x2d = x.reshape(n_batch * seq, D_MODEL)
        later = (w_in[l], w_attn_out[l], w_pool_group[l].reshape(POOL_WIDTH, POOL_GROUP_DIM),
                 w_pool_out[l], w_o[l], w_ffn2_in[l], w_ffn2_out[l])
        x2d, later = _ffn(x2d, mod, g_ffn1[l], w_ffn1_in_b, w_ffn1_out_b, mod_row=0,
                          tiles_per_batch=tiles_per_batch, cast=later)
        w_in_b, w_attn_out_b, w_pool_group_b, w_pool_out_b, w_o_b, w_ffn2_in_b, w_ffn2_out_b = later
        x = _token_mix(x2d.reshape(n_batch, seq, D_MODEL), mod, g_mix[l], w_in_b, q_gain[l], k_gain[l],
                       rel_bias[l], w_attn_out_b, w_pool_group_b, pool_scale[l], w_pool_out_b, w_o_b)
        x2d, _ = _ffn(x.reshape(n_batch * seq, D_MODEL), mod, g_ffn2[l], w_ffn2_in_b, w_ffn2_out_b,
                      mod_row=6, tiles_per_batch=tiles_per_batch)
        x = x2d.reshape(n_batch, seq, D_MODEL)
    return x
```

```python
import functools
import math

import jax
import jax.numpy as jnp
from jax import lax
from jax.experimental import pallas as pl
from jax.experimental.pallas import tpu as pltpu

D_MODEL = 1024
CHUNK = 64
LEFT_CHUNKS = 8
N_HEADS = 8
HEAD_DIM = 64
ATTN_WIDTH = N_HEADS * HEAD_DIM
POOL_WINDOWS = (2, 4, 8, 16)
POOL_GROUPS = len(POOL_WINDOWS)
POOL_WIDTH = 512
POOL_GROUP_DIM = POOL_WIDTH // POOL_GROUPS
REL_CLIP = 128
D_FF = 2816
N_ADA = 9
EPS = 1e-6
MASK_VALUE = -1e30
LOG2_E = math.log2(math.e)

LANES = 128
MXU_DIM = 256
F32_SUBLANES = 8
BF16_SUBLANES = 16
VMEM_LIMIT_BYTES = 56 * 1024 * 1024

ROW_TILE = 512
FFN_ROW_TILE = 1024
FF_CHUNK = MXU_DIM
N_FF_CHUNKS = D_FF // FF_CHUNK
BAND = LEFT_CHUNKS * CHUNK
Q_BLOCK = 2 * CHUNK
KEY_WINDOW = Q_BLOCK + BAND
QK_ROWS = 2 * Q_BLOCK
QK_SPAN = QK_ROWS + BAND
HEAD_PAIRS = ATTN_WIDTH // LANES
V_GROUP = 2 * LANES
BIAS_PERIOD = 768
POOL_HISTORY = 16
MOD_STEPS = 32
WOUT_STEPS = 16

assert D_FF % FF_CHUNK == 0
assert ROW_TILE == BAND
assert ROW_TILE % QK_ROWS == 0 and 2 * HEAD_DIM == LANES
assert BIAS_PERIOD >= Q_BLOCK + KEY_WINDOW - 1 and BIAS_PERIOD % LANES == 0
assert POOL_HISTORY >= max(POOL_WINDOWS) - 1 and POOL_HISTORY % F32_SUBLANES == 0
assert all(w & (w - 1) == 0 for w in POOL_WINDOWS) and list(POOL_WINDOWS) == sorted(POOL_WINDOWS)
assert D_MODEL % (MOD_STEPS * BF16_SUBLANES) == 0 and D_FF % (WOUT_STEPS * BF16_SUBLANES) == 0
assert WOUT_STEPS <= MOD_STEPS


def _resident(shape):
    zeros = (0,) * len(shape)
    return pl.BlockSpec(shape, lambda *_: zeros, pipeline_mode=pl.Buffered(1))


def _row_blockable(arr, n_blocks):
    rows, cols = arr.shape
    split = 1
    while (rows * split) % (n_blocks * BF16_SUBLANES) or (cols // split) % LANES:
        split *= 2
        assert cols % split == 0, (arr.shape, n_blocks)
    return arr.reshape(rows * split, cols // split)


def _norm_modulate(x, g, shift, scale):
    ms = jnp.mean(x * x, axis=-1, keepdims=True)
    col_gain = g * (1.0 + scale)
    return (x * lax.rsqrt(ms + EPS) * col_gain + shift).astype(jnp.bfloat16)


def _swiglu_rows(x, g, shift, scale, gate, win_ref, wout_ref):
    h = _norm_modulate(x, g, shift, scale)
    acts = []
    for c in range(N_FF_CHUNKS):
        lo = c * FF_CHUNK
        a = jnp.dot(h, win_ref[:, lo:lo + FF_CHUNK], preferred_element_type=jnp.float32)
        b = jnp.dot(h, win_ref[:, D_FF + lo:D_FF + lo + FF_CHUNK], preferred_element_type=jnp.float32)
        acts.append((a * jax.nn.sigmoid(a) * b).astype(jnp.bfloat16))
    acc = jnp.dot(jnp.concatenate(acts, axis=1), wout_ref[...], preferred_element_type=jnp.float32)
    return x + (0.5 * gate) * acc


def _first_kernel(ct_ref, wada_ref, bada_ref, win32_ref, wout32_ref, x_ref, g_ref, *rest,
                  n_cast, tiles_per_batch):
    cast_src = rest[:n_cast]
    mod_ref, o_ref = rest[n_cast], rest[n_cast + 1]
    cast_dst = rest[n_cast + 2:2 * n_cast + 2]
    win_buf, wout_buf = rest[2 * n_cast + 2:]
    step = pl.program_id(0)
    n_batch, n_ada, width = mod_ref.shape

    @pl.when(step < MOD_STEPS)
    def _():
        @pl.when(step == 0)
        def _():
            for j in range(n_ada):
                mod_ref[:, j, :] = jnp.broadcast_to(bada_ref[:, j * width:(j + 1) * width], (n_batch, width))

        for b in range(n_batch):
            col = ct_ref[:, b:b + 1]
            col = col * jax.nn.sigmoid(col)
            total = jnp.sum(wada_ref[...] * col, axis=0, keepdims=True)
            for j in range(n_ada):
                mod_ref[b, j:j + 1, :] += total[:, j * width:(j + 1) * width]
        rows = win32_ref.shape[0]
        win_buf[pl.ds(pl.multiple_of(step * rows, rows), rows), :] = win32_ref[...].astype(win_buf.dtype)

        @pl.when(step < WOUT_STEPS)
        def _():
            rows_out = wout32_ref.shape[0]
            wout_buf[pl.ds(pl.multiple_of(step * rows_out, rows_out), rows_out), :] = (
                wout32_ref[...].astype(wout_buf.dtype))

    @pl.when(step >= MOD_STEPS)
    def _():
        mod = mod_ref[(step - MOD_STEPS) // tiles_per_batch]
        o_ref[...] = _swiglu_rows(x_ref[...], g_ref[...], mod[0:1], mod[1:2], mod[2:3], win_buf, wout_buf)
        for src, dst in zip(cast_src, cast_dst):
            dst[...] = src[...].astype(dst.dtype)


def _modulated_first_ffn(x2d, c, w_ada, b_ada, g, w_in, w_out, *, tiles_per_batch, cast=()):
    n_batch = c.shape[0]
    n_cols = w_ada.shape[1]
    n_tiles = x2d.shape[0] // FFN_ROW_TILE
    cast_shapes = [arr.shape for arr in cast]
    cast = [_row_blockable(arr, n_tiles) for arr in cast]

    def lead(blocks):
        return lambda s: (jnp.minimum(s, blocks - 1), 0)

    def tile(s):
        return (jnp.maximum(s - MOD_STEPS, 0), 0)

    cast_specs = [pl.BlockSpec((v.shape[0] // n_tiles, v.shape[1]), tile) for v in cast]
    out = pl.pallas_call(
        functools.partial(_first_kernel, n_cast=len(cast), tiles_per_batch=tiles_per_batch),
        grid=(MOD_STEPS + n_tiles,),
        in_specs=[
            pl.BlockSpec((D_MODEL // MOD_STEPS, n_batch), lead(MOD_STEPS)),
            pl.BlockSpec((D_MODEL // MOD_STEPS, n_cols), lead(MOD_STEPS)),
            pl.BlockSpec((1, n_cols), lambda s: (0, 0)),
            pl.BlockSpec((w_in.shape[0] // MOD_STEPS, w_in.shape[1]), lead(MOD_STEPS)),
            pl.BlockSpec((w_out.shape[0] // WOUT_STEPS, w_out.shape[1]), lead(WOUT_STEPS)),
            pl.BlockSpec((FFN_ROW_TILE, D_MODEL), tile),
            _resident((1, D_MODEL)),
            *cast_specs,
        ],
        out_specs=[pl.BlockSpec((n_batch, N_ADA, D_MODEL), lambda s: (0, 0, 0)),
                   pl.BlockSpec((FFN_ROW_TILE, D_MODEL), tile), *cast_specs],
        out_shape=[jax.ShapeDtypeStruct((n_batch, N_ADA, D_MODEL), jnp.float32),
                   jax.ShapeDtypeStruct(x2d.shape, x2d.dtype),
                   *[jax.ShapeDtypeStruct(arr.shape, jnp.bfloat16) for arr in cast]],
        scratch_shapes=[pltpu.VMEM(w_in.shape, jnp.bfloat16), pltpu.VMEM(w_out.shape, jnp.bfloat16)],
        compiler_params=pltpu.CompilerParams(
            dimension_semantics=("arbitrary",), vmem_limit_bytes=VMEM_LIMIT_BYTES),
        name="modulation_and_swiglu",
    )(c.T, w_ada, b_ada.reshape(1, n_cols), w_in, w_out, x2d, g.reshape(1, D_MODEL), *cast)
    return out[1], out[0], [arr.reshape(shape) for arr, shape in zip(out[2:], cast_shapes)]


def _ffn_kernel(x_ref, mod_ref, g_ref, win_ref, wout_ref, o_ref, *, mod_row):
    o_ref[...] = _swiglu_rows(x_ref[...], g_ref[...], mod_ref[0, mod_row:mod_row + 1, :],
                              mod_ref[0, mod_row + 1:mod_row + 2, :], mod_ref[0, mod_row + 2:mod_row + 3, :],
                              win_ref, wout_ref)


def _ffn(x2d, mod, g, w_in, w_out, *, mod_row, tiles_per_batch):
    n_steps = x2d.shape[0] // FFN_ROW_TILE
    return pl.pallas_call(
        functools.partial(_ffn_kernel, mod_row=mod_row),
        grid=(n_steps,),
        in_specs=[
            pl.BlockSpec((FFN_ROW_TILE, D_MODEL), lambda i: (i, 0)),
            pl.BlockSpec((1, N_ADA, D_MODEL), lambda i: (i // tiles_per_batch, 0, 0)),
            _resident((1, D_MODEL)),
            _resident((D_MODEL, 2 * D_FF)),
            _resident((D_FF, D_MODEL)),
        ],
        out_specs=pl.BlockSpec((FFN_ROW_TILE, D_MODEL), lambda i: (i, 0)),
        out_shape=jax.ShapeDtypeStruct(x2d.shape, x2d.dtype),
        compiler_params=pltpu.CompilerParams(
            dimension_semantics=("arbitrary",), vmem_limit_bytes=VMEM_LIMIT_BYTES),
        name="swiglu_sublayer",
    )(x2d, mod, g.reshape(1, D_MODEL), w_in, w_out)


def _head_rms_norm(t, head_gain):
    lane = lax.broadcasted_iota(jnp.int32, (t.shape[0], LANES), 1)
    low = lane < HEAD_DIM
    pair_gain = jnp.concatenate([head_gain, head_gain], axis=1)
    parts = []
    for p in range(HEAD_PAIRS):
        blk = t[:, p * LANES:(p + 1) * LANES]
        sq = blk * blk
        r_low = lax.rsqrt(jnp.sum(jnp.where(low, sq, 0.0), axis=-1, keepdims=True) * (1.0 / HEAD_DIM) + EPS)
        r_high = lax.rsqrt(jnp.sum(jnp.where(low, 0.0, sq), axis=-1, keepdims=True) * (1.0 / HEAD_DIM) + EPS)
        scaled = blk * jnp.where(low, r_low, r_high) * pair_gain
        parts.append(scaled.astype(jnp.bfloat16))
    return jnp.concatenate(parts, axis=-1)


def _build_bias_table(diag_ref, bias_buf):
    r = lax.broadcasted_iota(jnp.int32, (Q_BLOCK, BIAS_PERIOD), 0)
    j = lax.broadcasted_iota(jnp.int32, (Q_BLOCK, BIAS_PERIOD), 1)
    rel_chunk = r // CHUNK + LEFT_CHUNKS - j // CHUNK
    in_band = jnp.logical_and(rel_chunk >= 0, rel_chunk <= LEFT_CHUNKS)
    for head in range(N_HEADS):
        diag = jnp.broadcast_to(diag_ref[head:head + 1, :], (Q_BLOCK, BIAS_PERIOD))
        table = pltpu.roll(diag, 0, 1, stride=1, stride_axis=0)
        table = jnp.where(in_band, table * LOG2_E, MASK_VALUE)
        row0 = (head % 2) * Q_BLOCK
        bias_buf[head // 2, row0:row0 + Q_BLOCK, :] = table[:, :KEY_WINDOW]
    bias_buf[HEAD_PAIRS:2 * HEAD_PAIRS] = jnp.full((HEAD_PAIRS, 2 * Q_BLOCK, KEY_WINDOW), MASK_VALUE,
                                                   bias_buf.dtype)


def _mix_kernel(x_ref, mod_ref, g_ref, win_ref, qg_ref, kg_ref, diag_ref, wa_ref, wg_ref,
                ps_ref, wp_ref, wo_ref, o_ref, k_buf, v_buf, u_buf, a_buf, bias_buf, wpool_buf):
    tile = pl.program_id(1)
    first = tile == 0

    @pl.when(jnp.logical_and(pl.program_id(0) == 0, first))
    def _():
        _build_bias_table(diag_ref, bias_buf)
        for gi in range(POOL_GROUPS):
            rows = slice(gi * POOL_GROUP_DIM, (gi + 1) * POOL_GROUP_DIM)
            scaled = (wg_ref[rows, :].astype(jnp.float32) * ps_ref[:, rows]).astype(jnp.bfloat16)
            wpool_buf[rows, :] = jnp.dot(scaled, wp_ref[rows, :],
                                         preferred_element_type=jnp.float32).astype(wpool_buf.dtype)
        for p in range(HEAD_PAIRS):
            v_buf[ROW_TILE:2 * ROW_TILE, p * V_GROUP + LANES:(p + 1) * V_GROUP] = jnp.ones(
                (ROW_TILE, LANES), v_buf.dtype)

    @pl.when(first)
    def _():
        k_buf[0:ROW_TILE, :] = jnp.zeros((ROW_TILE, ATTN_WIDTH), k_buf.dtype)
        v_buf[0:ROW_TILE, :] = jnp.zeros((ROW_TILE, HEAD_PAIRS * V_GROUP), v_buf.dtype)
        u_buf[0:POOL_HISTORY, :] = jnp.zeros((POOL_HISTORY, POOL_WIDTH), u_buf.dtype)

    x = x_ref[...]
    shift = mod_ref[0, 3:4, :]
    scale = mod_ref[0, 4:5, :]
    gate = mod_ref[0, 5:6, :]
    h = _norm_modulate(x, g_ref[...], shift, scale)

    def proj(col0, width):
        return jnp.dot(h, win_ref[:, col0:col0 + width], preferred_element_type=jnp.float32)

    u = proj(3 * ATTN_WIDTH, POOL_WIDTH)
    u_buf[POOL_HISTORY:POOL_HISTORY + ROW_TILE, :] = u
    pos = tile * ROW_TILE + lax.broadcasted_iota(jnp.int32, (ROW_TILE, 1), 0)
    sums = u_buf[...]
    pooled = []
    span = 1
    for gi, window in enumerate(POOL_WINDOWS):
        while span < window:
            sums = sums + pltpu.roll(sums, span, axis=0)
            span *= 2
        total = sums[POOL_HISTORY:, :POOL_GROUP_DIM]
        if gi + 1 < POOL_GROUPS:
            sums = sums[:, POOL_GROUP_DIM:]
        count = jnp.minimum(pos + 1, window).astype(jnp.float32)
        u_group = u[:, gi * POOL_GROUP_DIM:(gi + 1) * POOL_GROUP_DIM]
        pooled.append((total / count - u_group).astype(jnp.bfloat16))
    pooled = jnp.concatenate(pooled, axis=-1)
    u_buf[0:POOL_HISTORY, :] = u_buf[ROW_TILE:ROW_TILE + POOL_HISTORY, :]

    q = proj(0, ATTN_WIDTH)
    qn = _head_rms_norm(q, qg_ref[...] * (HEAD_DIM ** -0.5 * LOG2_E))
    k = proj(ATTN_WIDTH, ATTN_WIDTH)
    k_buf[ROW_TILE:2 * ROW_TILE, :] = _head_rms_norm(k, kg_ref[...])
    v = proj(2 * ATTN_WIDTH, ATTN_WIDTH)
    for p in range(HEAD_PAIRS):
        v_buf[ROW_TILE:2 * ROW_TILE, p * V_GROUP:p * V_GROUP + LANES] = (
            v[:, p * LANES:(p + 1) * LANES].astype(jnp.bfloat16))

    lane = lax.broadcasted_iota(jnp.int32, (Q_BLOCK, LANES), 1)
    low = lane < HEAD_DIM
    masked_table = first.astype(jnp.int32) * HEAD_PAIRS

    def scores(half, p):
        base = half * QK_ROWS
        cols = slice(p * LANES, (p + 1) * LANES)
        stack = []
        for blk in range(QK_ROWS // Q_BLOCK):
            q_pair = qn[base + blk * Q_BLOCK:base + (blk + 1) * Q_BLOCK, cols]
            zero = jnp.zeros_like(q_pair)
            stack += [jnp.where(low, q_pair, zero), jnp.where(low, zero, q_pair)]
        return lax.dot_general(jnp.concatenate(stack, axis=0), k_buf[base:base + QK_SPAN, cols],
                               (((1,), (1,)), ((), ())),
                               preferred_element_type=jnp.float32)

    def attend(half, p, s_all):
        cols = slice(p * LANES, (p + 1) * LANES)
        for blk in range(QK_ROWS // Q_BLOCK):
            row0 = half * QK_ROWS + blk * Q_BLOCK
            bias = jnp.concatenate(
                [bias_buf[(masked_table if c0 < ROW_TILE - row0 else 0) + p, :, c0:c0 + LANES]
                 for c0 in range(0, KEY_WINDOW, LANES)], axis=1)
            s = s_all[2 * blk * Q_BLOCK:2 * (blk + 1) * Q_BLOCK,
                      blk * Q_BLOCK:blk * Q_BLOCK + KEY_WINDOW] + bias
            m = jnp.max(s, axis=-1, keepdims=True)
            pexp = jnp.exp2(s - m).astype(jnp.bfloat16)
            o = jnp.dot(pexp, v_buf[row0:row0 + KEY_WINDOW, p * V_GROUP:(p + 1) * V_GROUP],
                        preferred_element_type=jnp.float32)
            o = o[:, :LANES] / o[:, LANES:]
            a_buf[row0:row0 + Q_BLOCK, cols] = jnp.where(
                low, o[:Q_BLOCK], o[Q_BLOCK:]).astype(a_buf.dtype)

    units = [(half, p) for half in range(ROW_TILE // QK_ROWS) for p in range(HEAD_PAIRS)]
    gate_col0 = 3 * ATTN_WIDTH + POOL_WIDTH
    gate_width = 2 * D_MODEL // len(units)
    gate_parts = [proj(gate_col0, gate_width)]
    s_next = scores(*units[0])
    for n, unit in enumerate(units):
        s_cur = s_next
        if n + 1 < len(units):
            s_next = scores(*units[n + 1])
            gate_parts.append(proj(gate_col0 + (n + 1) * gate_width, gate_width))
        attend(*unit, s_cur)
    gates = jnp.concatenate(gate_parts, axis=1)
    ya = jnp.dot(a_buf[...], wa_ref[...], preferred_element_type=jnp.float32)
    k_buf[0:ROW_TILE, :] = k_buf[ROW_TILE:2 * ROW_TILE, :]
    v_buf[0:ROW_TILE, :] = v_buf[ROW_TILE:2 * ROW_TILE, :]

    yb = jnp.dot(pooled, wpool_buf[...], preferred_element_type=jnp.float32)

    merged = (jax.nn.sigmoid(gates[:, :D_MODEL]) * ya
              + jax.nn.sigmoid(gates[:, D_MODEL:]) * yb).astype(jnp.bfloat16)
    o_ref[...] = x + gate * jnp.dot(merged, wo_ref[...], preferred_element_type=jnp.float32)


def _bias_diagonals(rel_bias):
    n_far = BAND - REL_CLIP + 1
    n_near = KEY_WINDOW - n_far
    far = rel_bias[:, 2 * REL_CLIP:]
    near = rel_bias[:, 2 * REL_CLIP - n_near:2 * REL_CLIP][:, ::-1]
    return jnp.concatenate(
        [jnp.broadcast_to(far, (N_HEADS, n_far)), near,
         jnp.broadcast_to(far, (N_HEADS, BIAS_PERIOD - KEY_WINDOW))], axis=1).astype(jnp.float32)


def _token_mix(x3d, mod, g, w_in, q_gain, k_gain, rel_bias, w_attn_out, w_pool_group, pool_scale,
               w_pool_out, w_o):
    n_batch, seq, _ = x3d.shape
    bf16 = jnp.bfloat16
    in_cols = w_in.shape[1]
    return pl.pallas_call(
        _mix_kernel,
        grid=(n_batch, seq // ROW_TILE),
        in_specs=[
            pl.BlockSpec((None, ROW_TILE, D_MODEL), lambda b, i: (b, i, 0)),
            pl.BlockSpec((1, N_ADA, D_MODEL), lambda b, i: (b, 0, 0)),
            _resident((1, D_MODEL)),
            _resident((D_MODEL, in_cols)),
            _resident((1, HEAD_DIM)),
            _resident((1, HEAD_DIM)),
            _resident((N_HEADS, BIAS_PERIOD)),
            _resident((ATTN_WIDTH, D_MODEL)),
            _resident((POOL_WIDTH, POOL_GROUP_DIM)),
            _resident((1, POOL_WIDTH)),
            _resident((POOL_WIDTH, D_MODEL)),
            _resident((D_MODEL, D_MODEL)),
        ],
        out_specs=pl.BlockSpec((None, ROW_TILE, D_MODEL), lambda b, i: (b, i, 0)),
        out_shape=jax.ShapeDtypeStruct(x3d.shape, x3d.dtype),
        scratch_shapes=[
            pltpu.VMEM((2 * ROW_TILE, ATTN_WIDTH), bf16),
            pltpu.VMEM((2 * ROW_TILE, HEAD_PAIRS * V_GROUP), bf16),
            pltpu.VMEM((POOL_HISTORY + ROW_TILE, POOL_WIDTH), jnp.float32),
            pltpu.VMEM((ROW_TILE, ATTN_WIDTH), bf16),
            pltpu.VMEM((2 * HEAD_PAIRS, 2 * Q_BLOCK, KEY_WINDOW), jnp.float32),
            pltpu.VMEM((POOL_WIDTH, D_MODEL), bf16),
        ],
        compiler_params=pltpu.CompilerParams(
            dimension_semantics=("arbitrary", "arbitrary"), vmem_limit_bytes=VMEM_LIMIT_BYTES),
        name="token_mix_sublayer",
    )(x3d, mod, g.reshape(1, D_MODEL), w_in,
      q_gain.reshape(1, HEAD_DIM), k_gain.reshape(1, HEAD_DIM),
      _bias_diagonals(rel_bias), w_attn_out, w_pool_group, pool_scale.reshape(1, POOL_WIDTH), w_pool_out, w_o)


def kernel(x, c, w_ada, b_ada, g_ffn1, w_ffn1_in, w_ffn1_out, g_mix, w_in, q_gain, k_gain, rel_bias,
           w_attn_out, w_pool_group, pool_scale, w_pool_out, w_o, g_ffn2, w_ffn2_in, w_ffn2_out):
    n_batch, seq, d_model = x.shape
    depth = w_ada.shape[0]
    assert d_model == D_MODEL and seq % ROW_TILE == 0 and seq % FFN_ROW_TILE == 0
    tiles_per_batch = seq // FFN_ROW_TILE
    for l in range(depth):
        x2d = x.reshape(n_batch * seq, D_MODEL)
        later = (w_in[l], w_attn_out[l], w_pool_group[l].reshape(POOL_WIDTH, POOL_GROUP_DIM),
                 w_pool_out[l], w_o[l], w_ffn2_in[l], w_ffn2_out[l])
        x2d, mod, later = _modulated_first_ffn(x2d, c, w_ada[l], b_ada[l], g_ffn1[l], w_ffn1_in[l],
                                               w_ffn1_out[l], tiles_per_batch=tiles_per_batch, cast=later)
        w_in_b, w_attn_out_b, w_pool_group_b, w_pool_out_b, w_o_b, w_ffn2_in_b, w_ffn2_out_b = later
        x = _token_mix(x2d.reshape(n_batch, seq, D_MODEL), mod, g_mix[l], w_in_b, q_gain[l], k_gain[l],
                       rel_bias[l], w_attn_out_b, w_pool_group_b, pool_scale[l], w_pool_out_b, w_o_b)
        x2d = _ffn(x.reshape(n_batch * seq, D_MODEL), mod, g_ffn2[l], w_ffn2_in_b, w_ffn2_out_b,
                   mod_row=6, tiles_per_batch=tiles_per_batch)
        x = x2d.reshape(n_batch, seq, D_MODEL)
    return x
```

```python
import functools
import math

import jax
import jax.numpy as jnp
from jax import lax
from jax.experimental import pallas as pl
from jax.experimental.pallas import tpu as pltpu

D_MODEL = 1024
CHUNK = 64
LEFT_CHUNKS = 8
N_HEADS = 8
HEAD_DIM = 64
ATTN_WIDTH = N_HEADS * HEAD_DIM
POOL_WINDOWS = (2, 4, 8, 16)
POOL_GROUPS = len(POOL_WINDOWS)
POOL_WIDTH = 512
POOL_GROUP_DIM = POOL_WIDTH // POOL_GROUPS
REL_CLIP = 128
D_FF = 2816
N_ADA = 9
EPS = 1e-6
MASK_VALUE = -1e30
LOG2_E = math.log2(math.e)

LANES = 128
MXU_DIM = 256
F32_SUBLANES = 8
BF16_SUBLANES = 16
VMEM_LIMIT_BYTES = 56 * 1024 * 1024

ROW_TILE = 512
FFN_ROW_TILE = 1024
FF_CHUNK = MXU_DIM
N_FF_CHUNKS = D_FF // FF_CHUNK
BAND = LEFT_CHUNKS * CHUNK
Q_BLOCK = 2 * CHUNK
KEY_WINDOW = Q_BLOCK + BAND
QK_ROWS = 2 * Q_BLOCK
QK_SPAN = QK_ROWS + BAND
HEAD_PAIRS = ATTN_WIDTH // LANES
V_GROUP = 2 * LANES
BIAS_PERIOD = 768
POOL_HISTORY = 16
MOD_STEPS = 8

assert D_FF % FF_CHUNK == 0
assert ROW_TILE == BAND
assert ROW_TILE % QK_ROWS == 0 and 2 * HEAD_DIM == LANES
assert BIAS_PERIOD >= Q_BLOCK + KEY_WINDOW - 1 and BIAS_PERIOD % LANES == 0
assert POOL_HISTORY >= max(POOL_WINDOWS) - 1 and POOL_HISTORY % F32_SUBLANES == 0
assert all(w & (w - 1) == 0 for w in POOL_WINDOWS) and list(POOL_WINDOWS) == sorted(POOL_WINDOWS)
assert D_MODEL % (MOD_STEPS * F32_SUBLANES) == 0


def _resident(shape):
    zeros = (0,) * len(shape)
    return pl.BlockSpec(shape, lambda *_: zeros, pipeline_mode=pl.Buffered(1))


def _row_blockable(arr, n_blocks):
    rows, cols = arr.shape
    split = 1
    while (rows * split) % (n_blocks * BF16_SUBLANES) or (cols // split) % LANES:
        split *= 2
        assert cols % split == 0, (arr.shape, n_blocks)
    return arr.reshape(rows * split, cols // split)


def _cast_specs(cast, n_blocks):
    views = [_row_blockable(arr, n_blocks) for arr in cast]
    specs = [pl.BlockSpec((v.shape[0] // n_blocks, v.shape[1]), lambda i: (i, 0)) for v in views]
    return views, specs


def _mod_kernel(ct_ref, w_ref, b_ref, *rest, n_cast):
    cast_src, o_ref, cast_dst = rest[:n_cast], rest[n_cast], rest[n_cast + 1:]

    n_batch, n_ada, width = o_ref.shape

    @pl.when(pl.program_id(0) == 0)
    def _():
        for j in range(n_ada):
            o_ref[:, j, :] = jnp.broadcast_to(b_ref[:, j * width:(j + 1) * width], (n_batch, width))

    for b in range(n_batch):
        col = ct_ref[:, b:b + 1]
        col = col * jax.nn.sigmoid(col)
        total = jnp.sum(w_ref[...] * col, axis=0, keepdims=True)
        for j in range(n_ada):
            o_ref[b, j:j + 1, :] += total[:, j * width:(j + 1) * width]
    for src, dst in zip(cast_src, cast_dst):
        dst[...] = src[...].astype(dst.dtype)


def _modulation(c, w_ada, b_ada, cast=()):
    n_batch = c.shape[0]
    n_cols = w_ada.shape[1]
    rows = D_MODEL // MOD_STEPS
    cast_shapes = [arr.shape for arr in cast]
    cast, cast_specs = _cast_specs(cast, MOD_STEPS)
    out = pl.pallas_call(
        functools.partial(_mod_kernel, n_cast=len(cast)),
        grid=(MOD_STEPS,),
        in_specs=[
            pl.BlockSpec((rows, n_batch), lambda j: (j, 0)),
            pl.BlockSpec((rows, n_cols), lambda j: (j, 0)),
            pl.BlockSpec((1, n_cols), lambda j: (0, 0)),
            *cast_specs,
        ],
        out_specs=[pl.BlockSpec((n_batch, N_ADA, D_MODEL), lambda j: (0, 0, 0)), *cast_specs],
        out_shape=[jax.ShapeDtypeStruct((n_batch, N_ADA, D_MODEL), jnp.float32),
                   *[jax.ShapeDtypeStruct(arr.shape, jnp.bfloat16) for arr in cast]],
        compiler_params=pltpu.CompilerParams(
            dimension_semantics=("arbitrary",), vmem_limit_bytes=VMEM_LIMIT_BYTES),
        name="adaln_modulation",
    )(c.T, w_ada, b_ada.reshape(1, n_cols), *cast)
    return out[0], [arr.reshape(shape) for arr, shape in zip(out[1:], cast_shapes)]


def _norm_modulate(x, g, shift, scale):
    ms = jnp.mean(x * x, axis=-1, keepdims=True)
    col_gain = g * (1.0 + scale)
    return (x * lax.rsqrt(ms + EPS) * col_gain + shift).astype(jnp.bfloat16)


def _swiglu_tile(x, g, shift, scale, gate, win_ref, wout_ref):
    h = _norm_modulate(x, g, shift, scale)
    acts = []
    for c in range(N_FF_CHUNKS):
        lo = c * FF_CHUNK
        a = jnp.dot(h, win_ref[:, lo:lo + FF_CHUNK], preferred_element_type=jnp.float32)
        b = jnp.dot(h, win_ref[:, D_FF + lo:D_FF + lo + FF_CHUNK], preferred_element_type=jnp.float32)
        acts.append((a * jax.nn.sigmoid(a) * b).astype(jnp.bfloat16))
    acc = jnp.dot(jnp.concatenate(acts, axis=1), wout_ref[...], preferred_element_type=jnp.float32)
    return x + (0.5 * gate) * acc


def _ffn_kernel(x_ref, mod_ref, g_ref, win_ref, wout_ref, *rest, mod_row, n_cast):
    cast_src, o_ref, cast_dst = rest[:n_cast], rest[n_cast], rest[n_cast + 1:]
    o_ref[...] = _swiglu_tile(x_ref[...], g_ref[...], mod_ref[0, mod_row:mod_row + 1, :],
                              mod_ref[0, mod_row + 1:mod_row + 2, :], mod_ref[0, mod_row + 2:mod_row + 3, :],
                              win_ref, wout_ref)
    for src, dst in zip(cast_src, cast_dst):
        dst[...] = src[...].astype(dst.dtype)


def _ffn_streamed_kernel(x_hbm, modt_hbm, g_ref, win_ref, wout_ref, o_hbm):
    def tile_body(x_blk, modt_blk, o_blk):
        o_blk[...] = _swiglu_tile(x_blk[...], g_ref[...], modt_blk[0, 0:1, :], modt_blk[0, 1:2, :],
                                  modt_blk[0, 2:3, :], win_ref, wout_ref)

    n_tiles = x_hbm.shape[0] // FFN_ROW_TILE
    pltpu.emit_pipeline(
        tile_body,
        grid=(n_tiles,),
        in_specs=[pl.BlockSpec((FFN_ROW_TILE, D_MODEL), lambda i: (i, 0)),
                  pl.BlockSpec((1, 3, D_MODEL), lambda i: (i, 0, 0))],
        out_specs=[pl.BlockSpec((FFN_ROW_TILE, D_MODEL), lambda i: (i, 0))],
    )(x_hbm, modt_hbm, o_hbm)


def _ffn_streamed(x2d, mod, g, w_in, w_out, *, mod_row, tiles_per_batch):
    mod_tiles = jnp.repeat(mod[:, mod_row:mod_row + 3, :], tiles_per_batch, axis=0)
    whole_vmem = pl.BlockSpec(memory_space=pltpu.VMEM)
    return pl.pallas_call(
        _ffn_streamed_kernel,
        in_specs=[pl.BlockSpec(memory_space=pl.ANY), pl.BlockSpec(memory_space=pl.ANY),
                  whole_vmem, whole_vmem, whole_vmem],
        out_specs=pl.BlockSpec(memory_space=pl.ANY),
        out_shape=jax.ShapeDtypeStruct(x2d.shape, x2d.dtype),
        compiler_params=pltpu.CompilerParams(vmem_limit_bytes=VMEM_LIMIT_BYTES),
        name="swiglu_streamed",
    )(x2d, mod_tiles, g.reshape(1, D_MODEL), w_in, w_out)


def _ffn(x2d, mod, g, w_in, w_out, *, mod_row, tiles_per_batch, cast=()):
    n_rows = x2d.shape[0]
    n_steps = n_rows // FFN_ROW_TILE
    cast_shapes = [arr.shape for arr in cast]
    cast, cast_specs = _cast_specs(cast, n_steps)
    out = pl.pallas_call(
        functools.partial(_ffn_kernel, mod_row=mod_row, n_cast=len(cast)),
        grid=(n_steps,),
        in_specs=[
            pl.BlockSpec((FFN_ROW_TILE, D_MODEL), lambda i: (i, 0)),
            pl.BlockSpec((1, N_ADA, D_MODEL), lambda i: (i // tiles_per_batch, 0, 0)),
            _resident((1, D_MODEL)),
            _resident((D_MODEL, 2 * D_FF)),
            _resident((D_FF, D_MODEL)),
            *cast_specs,
        ],
        out_specs=[pl.BlockSpec((FFN_ROW_TILE, D_MODEL), lambda i: (i, 0)), *cast_specs],
        out_shape=[jax.ShapeDtypeStruct(x2d.shape, x2d.dtype),
                   *[jax.ShapeDtypeStruct(arr.shape, jnp.bfloat16) for arr in cast]],
        compiler_params=pltpu.CompilerParams(
            dimension_semantics=("arbitrary",), vmem_limit_bytes=VMEM_LIMIT_BYTES),
        name="swiglu_sublayer",
    )(x2d, mod, g.reshape(1, D_MODEL), w_in, w_out, *cast)
    return out[0], [arr.reshape(shape) for arr, shape in zip(out[1:], cast_shapes)]


def _head_rms_norm(t, head_gain):
    lane = lax.broadcasted_iota(jnp.int32, (t.shape[0], LANES), 1)
    low = lane < HEAD_DIM
    pair_gain = jnp.concatenate([head_gain, head_gain], axis=1)
    parts = []
    for p in range(HEAD_PAIRS):
        blk = t[:, p * LANES:(p + 1) * LANES]
        sq = blk * blk
        r_low = lax.rsqrt(jnp.sum(jnp.where(low, sq, 0.0), axis=-1, keepdims=True) * (1.0 / HEAD_DIM) + EPS)
        r_high = lax.rsqrt(jnp.sum(jnp.where(low, 0.0, sq), axis=-1, keepdims=True) * (1.0 / HEAD_DIM) + EPS)
        scaled = blk * jnp.where(low, r_low, r_high) * pair_gain
        parts.append(scaled.astype(jnp.bfloat16))
    return jnp.concatenate(parts, axis=-1)


def _build_bias_table(diag_ref, bias_buf):
    r = lax.broadcasted_iota(jnp.int32, (Q_BLOCK, BIAS_PERIOD), 0)
    j = lax.broadcasted_iota(jnp.int32, (Q_BLOCK, BIAS_PERIOD), 1)
    rel_chunk = r // CHUNK + LEFT_CHUNKS - j // CHUNK
    in_band = jnp.logical_and(rel_chunk >= 0, rel_chunk <= LEFT_CHUNKS)
    for head in range(N_HEADS):
        diag = jnp.broadcast_to(diag_ref[head:head + 1, :], (Q_BLOCK, BIAS_PERIOD))
        table = pltpu.roll(diag, 0, 1, stride=1, stride_axis=0)
        table = jnp.where(in_band, table * LOG2_E, MASK_VALUE)
        row0 = (head % 2) * Q_BLOCK
        bias_buf[head // 2, row0:row0 + Q_BLOCK, :] = table[:, :KEY_WINDOW]
    bias_buf[HEAD_PAIRS:2 * HEAD_PAIRS] = jnp.full((HEAD_PAIRS, 2 * Q_BLOCK, KEY_WINDOW), MASK_VALUE,
                                                   bias_buf.dtype)


def _mix_kernel(x_ref, mod_ref, g_ref, win_ref, qg_ref, kg_ref, diag_ref, wa_ref, wg_ref,
                ps_ref, wp_ref, wo_ref, o_ref, k_buf, v_buf, u_buf, a_buf, bias_buf, wpool_buf):
    tile = pl.program_id(1)
    first = tile == 0

    @pl.when(jnp.logical_and(pl.program_id(0) == 0, first))
    def _():
        _build_bias_table(diag_ref, bias_buf)
        for gi in range(POOL_GROUPS):
            rows = slice(gi * POOL_GROUP_DIM, (gi + 1) * POOL_GROUP_DIM)
            scaled = (wg_ref[rows, :].astype(jnp.float32) * ps_ref[:, rows]).astype(jnp.bfloat16)
            wpool_buf[rows, :] = jnp.dot(scaled, wp_ref[rows, :],
                                         preferred_element_type=jnp.float32).astype(wpool_buf.dtype)
        for p in range(HEAD_PAIRS):
            v_buf[ROW_TILE:2 * ROW_TILE, p * V_GROUP + LANES:(p + 1) * V_GROUP] = jnp.ones(
                (ROW_TILE, LANES), v_buf.dtype)

    @pl.when(first)
    def _():
        k_buf[0:ROW_TILE, :] = jnp.zeros((ROW_TILE, ATTN_WIDTH), k_buf.dtype)
        v_buf[0:ROW_TILE, :] = jnp.zeros((ROW_TILE, HEAD_PAIRS * V_GROUP), v_buf.dtype)
        u_buf[0:POOL_HISTORY, :] = jnp.zeros((POOL_HISTORY, POOL_WIDTH), u_buf.dtype)

    x = x_ref[...]
    shift = mod_ref[0, 3:4, :]
    scale = mod_ref[0, 4:5, :]
    gate = mod_ref[0, 5:6, :]
    h = _norm_modulate(x, g_ref[...], shift, scale)

    def proj(col0, width):
        return jnp.dot(h, win_ref[:, col0:col0 + width], preferred_element_type=jnp.float32)

    u = proj(3 * ATTN_WIDTH, POOL_WIDTH)
    u_buf[POOL_HISTORY:POOL_HISTORY + ROW_TILE, :] = u
    pos = tile * ROW_TILE + lax.broadcasted_iota(jnp.int32, (ROW_TILE, 1), 0)
    sums = u_buf[...]
    pooled = []
    span = 1
    for gi, window in enumerate(POOL_WINDOWS):
        while span < window:
            sums = sums + pltpu.roll(sums, span, axis=0)
            span *= 2
        total = sums[POOL_HISTORY:, :POOL_GROUP_DIM]
        if gi + 1 < POOL_GROUPS:
            sums = sums[:, POOL_GROUP_DIM:]
        count = jnp.minimum(pos + 1, window).astype(jnp.float32)
        u_group = u[:, gi * POOL_GROUP_DIM:(gi + 1) * POOL_GROUP_DIM]
        pooled.append((total / count - u_group).astype(jnp.bfloat16))
    pooled = jnp.concatenate(pooled, axis=-1)
    u_buf[0:POOL_HISTORY, :] = u_buf[ROW_TILE:ROW_TILE + POOL_HISTORY, :]

    q = proj(0, ATTN_WIDTH)
    qn = _head_rms_norm(q, qg_ref[...] * (HEAD_DIM ** -0.5 * LOG2_E))
    k = proj(ATTN_WIDTH, ATTN_WIDTH)
    k_buf[ROW_TILE:2 * ROW_TILE, :] = _head_rms_norm(k, kg_ref[...])
    v = proj(2 * ATTN_WIDTH, ATTN_WIDTH)
    for p in range(HEAD_PAIRS):
        v_buf[ROW_TILE:2 * ROW_TILE, p * V_GROUP:p * V_GROUP + LANES] = (
            v[:, p * LANES:(p + 1) * LANES].astype(jnp.bfloat16))

    lane = lax.broadcasted_iota(jnp.int32, (Q_BLOCK, LANES), 1)
    low = lane < HEAD_DIM
    masked_table = first.astype(jnp.int32) * HEAD_PAIRS

    def scores(half, p):
        base = half * QK_ROWS
        cols = slice(p * LANES, (p + 1) * LANES)
        stack = []
        for blk in range(QK_ROWS // Q_BLOCK):
            q_pair = qn[base + blk * Q_BLOCK:base + (blk + 1) * Q_BLOCK, cols]
            zero = jnp.zeros_like(q_pair)
            stack += [jnp.where(low, q_pair, zero), jnp.where(low, zero, q_pair)]
        return lax.dot_general(jnp.concatenate(stack, axis=0), k_buf[base:base + QK_SPAN, cols],
                               (((1,), (1,)), ((), ())),
                               preferred_element_type=jnp.float32)

    def attend(half, p, s_all):
        cols = slice(p * LANES, (p + 1) * LANES)
        for blk in range(QK_ROWS // Q_BLOCK):
            row0 = half * QK_ROWS + blk * Q_BLOCK
            bias = jnp.concatenate(
                [bias_buf[(masked_table if c0 < ROW_TILE - row0 else 0) + p, :, c0:c0 + LANES]
                 for c0 in range(0, KEY_WINDOW, LANES)], axis=1)
            s = s_all[2 * blk * Q_BLOCK:2 * (blk + 1) * Q_BLOCK,
                      blk * Q_BLOCK:blk * Q_BLOCK + KEY_WINDOW] + bias
            m = jnp.max(s, axis=-1, keepdims=True)
            pexp = jnp.exp2(s - m).astype(jnp.bfloat16)
            o = jnp.dot(pexp, v_buf[row0:row0 + KEY_WINDOW, p * V_GROUP:(p + 1) * V_GROUP],
                        preferred_element_type=jnp.float32)
            o = o[:, :LANES] / o[:, LANES:]
            a_buf[row0:row0 + Q_BLOCK, cols] = jnp.where(
                low, o[:Q_BLOCK], o[Q_BLOCK:]).astype(a_buf.dtype)

    units = [(half, p) for half in range(ROW_TILE // QK_ROWS) for p in range(HEAD_PAIRS)]
    gate_col0 = 3 * ATTN_WIDTH + POOL_WIDTH
    gate_width = 2 * D_MODEL // len(units)
    gate_parts = [proj(gate_col0, gate_width)]
    s_next = scores(*units[0])
    for n, unit in enumerate(units):
        s_cur = s_next
        if n + 1 < len(units):
            s_next = scores(*units[n + 1])
            gate_parts.append(proj(gate_col0 + (n + 1) * gate_width, gate_width))
        attend(*unit, s_cur)
    gates = jnp.concatenate(gate_parts, axis=1)
    ya = jnp.dot(a_buf[...], wa_ref[...], preferred_element_type=jnp.float32)
    k_buf[0:ROW_TILE, :] = k_buf[ROW_TILE:2 * ROW_TILE, :]
    v_buf[0:ROW_TILE, :] = v_buf[ROW_TILE:2 * ROW_TILE, :]

    yb = jnp.dot(pooled, wpool_buf[...], preferred_element_type=jnp.float32)

    merged = (jax.nn.sigmoid(gates[:, :D_MODEL]) * ya
              + jax.nn.sigmoid(gates[:, D_MODEL:]) * yb).astype(jnp.bfloat16)
    o_ref[...] = x + gate * jnp.dot(merged, wo_ref[...], preferred_element_type=jnp.float32)


def _bias_diagonals(rel_bias):
    n_far = BAND - REL_CLIP + 1
    n_near = KEY_WINDOW - n_far
    far = rel_bias[:, 2 * REL_CLIP:]
    near = rel_bias[:, 2 * REL_CLIP - n_near:2 * REL_CLIP][:, ::-1]
    return jnp.concatenate(
        [jnp.broadcast_to(far, (N_HEADS, n_far)), near,
         jnp.broadcast_to(far, (N_HEADS, BIAS_PERIOD - KEY_WINDOW))], axis=1).astype(jnp.float32)


def _token_mix(x3d, mod, g, w_in, q_gain, k_gain, rel_bias, w_attn_out, w_pool_group, pool_scale,
               w_pool_out, w_o):
    n_batch, seq, _ = x3d.shape
    bf16 = jnp.bfloat16
    in_cols = w_in.shape[1]
    return pl.pallas_call(
        _mix_kernel,
        grid=(n_batch, seq // ROW_TILE),
        in_specs=[
            pl.BlockSpec((None, ROW_TILE, D_MODEL), lambda b, i: (b, i, 0)),
            pl.BlockSpec((1, N_ADA, D_MODEL), lambda b, i: (b, 0, 0)),
            _resident((1, D_MODEL)),
            _resident((D_MODEL, in_cols)),
            _resident((1, HEAD_DIM)),
            _resident((1, HEAD_DIM)),
            _resident((N_HEADS, BIAS_PERIOD)),
            _resident((ATTN_WIDTH, D_MODEL)),
            _resident((POOL_WIDTH, POOL_GROUP_DIM)),
            _resident((1, POOL_WIDTH)),
            _resident((POOL_WIDTH, D_MODEL)),
            _resident((D_MODEL, D_MODEL)),
        ],
        out_specs=pl.BlockSpec((None, ROW_TILE, D_MODEL), lambda b, i: (b, i, 0)),
        out_shape=jax.ShapeDtypeStruct(x3d.shape, x3d.dtype),
        scratch_shapes=[
            pltpu.VMEM((2 * ROW_TILE, ATTN_WIDTH), bf16),
            pltpu.VMEM((2 * ROW_TILE, HEAD_PAIRS * V_GROUP), bf16),
            pltpu.VMEM((POOL_HISTORY + ROW_TILE, POOL_WIDTH), jnp.float32),
            pltpu.VMEM((ROW_TILE, ATTN_WIDTH), bf16),
            pltpu.VMEM((2 * HEAD_PAIRS, 2 * Q_BLOCK, KEY_WINDOW), jnp.float32),
            pltpu.VMEM((POOL_WIDTH, D_MODEL), bf16),
        ],
        compiler_params=pltpu.CompilerParams(
            dimension_semantics=("arbitrary", "arbitrary"), vmem_limit_bytes=VMEM_LIMIT_BYTES),
        name="token_mix_sublayer",
    )(x3d, mod, g.reshape(1, D_MODEL), w_in,
      q_gain.reshape(1, HEAD_DIM), k_gain.reshape(1, HEAD_DIM),
      _bias_diagonals(rel_bias), w_attn_out, w_pool_group, pool_scale.reshape(1, POOL_WIDTH), w_pool_out, w_o)


def kernel(x, c, w_ada, b_ada, g_ffn1, w_ffn1_in, w_ffn1_out, g_mix, w_in, q_gain, k_gain, rel_bias,
           w_attn_out, w_pool_group, pool_scale, w_pool_out, w_o, g_ffn2, w_ffn2_in, w_ffn2_out):
    n_batch, seq, d_model = x.shape
    depth = w_ada.shape[0]
    assert d_model == D_MODEL and seq % ROW_TILE == 0 and seq % FFN_ROW_TILE == 0
    tiles_per_batch = seq // FFN_ROW_TILE
    for l in range(depth):
        mod, (w_ffn1_in_b, w_ffn1_out_b) = _modulation(c, w_ada[l], b_ada[l],
                                                        cast=(w_ffn1_in[l], w_ffn1_out[l]))
        x2d = x.reshape(n_batch * seq, D_MODEL)
        later = (w_in[l], w_attn_out[l], w_pool_group[l].reshape(POOL_WIDTH, POOL_GROUP_DIM),
                 w_pool_out[l], w_o[l], w_ffn2_in[l], w_ffn2_out[l])
        x2d, later = _ffn(x2d, mod, g_ffn1[l], w_ffn1_in_b, w_ffn1_out_b, mod_row=0,
                          tiles_per_batch=tiles_per_batch, cast=later)
        w_in_b, w_attn_out_b, w_pool_group_b, w_pool_out_b, w_o_b, w_ffn2_in_b, w_ffn2_out_b = later
        x = _token_mix(x2d.reshape(n_batch, seq, D_MODEL), mod, g_mix[l], w_in_b, q_gain[l], k_gain[l],
                       rel_bias[l], w_attn_out_b, w_pool_group_b, pool_scale[l], w_pool_out_b, w_o_b)
        x2d = _ffn_streamed(x.reshape(n_batch * seq, D_MODEL), mod, g_ffn2[l], w_ffn2_in_b, w_ffn2_out_b,
                            mod_row=6, tiles_per_batch=tiles_per_batch)
        x = x2d.reshape(n_batch, seq, D_MODEL)
    return x
```

```python
import functools
import math

import jax
import jax.numpy as jnp
from jax import lax
from jax.experimental import pallas as pl
from jax.experimental.pallas import tpu as pltpu

D_MODEL = 1024
CHUNK = 64
LEFT_CHUNKS = 8
N_HEADS = 8
HEAD_DIM = 64
ATTN_WIDTH = N_HEADS * HEAD_DIM
POOL_WINDOWS = (2, 4, 8, 16)
POOL_GROUPS = len(POOL_WINDOWS)
POOL_WIDTH = 512
POOL_GROUP_DIM = POOL_WIDTH // POOL_GROUPS
REL_CLIP = 128
D_FF = 2816
N_ADA = 9
EPS = 1e-6
MASK_VALUE = -1e30
LOG2_E = math.log2(math.e)

LANES = 128
MXU_DIM = 256
F32_SUBLANES = 8
BF16_SUBLANES = 16
VMEM_LIMIT_BYTES = 56 * 1024 * 1024

ROW_TILE = 512
FFN_ROW_TILE = 1024
FF_CHUNK = MXU_DIM
N_FF_CHUNKS = D_FF // FF_CHUNK
BAND = LEFT_CHUNKS * CHUNK
Q_BLOCK = 2 * CHUNK
KEY_WINDOW = Q_BLOCK + BAND
QK_ROWS = 2 * Q_BLOCK
QK_SPAN = QK_ROWS + BAND
HEAD_PAIRS = ATTN_WIDTH // LANES
V_GROUP = 2 * LANES
BIAS_PERIOD = 768
POOL_HISTORY = 16
MOD_STEPS = 16

assert D_FF % FF_CHUNK == 0
assert ROW_TILE == BAND
assert ROW_TILE % QK_ROWS == 0 and 2 * HEAD_DIM == LANES
assert BIAS_PERIOD >= Q_BLOCK + KEY_WINDOW - 1 and BIAS_PERIOD % LANES == 0
assert POOL_HISTORY >= max(POOL_WINDOWS) - 1 and POOL_HISTORY % F32_SUBLANES == 0
assert all(w & (w - 1) == 0 for w in POOL_WINDOWS) and list(POOL_WINDOWS) == sorted(POOL_WINDOWS)
assert D_MODEL % (MOD_STEPS * F32_SUBLANES) == 0


def _resident(shape):
    zeros = (0,) * len(shape)
    return pl.BlockSpec(shape, lambda *_: zeros, pipeline_mode=pl.Buffered(1))


def _row_blockable(arr, n_blocks):
    rows, cols = arr.shape
    split = 1
    while (rows * split) % (n_blocks * BF16_SUBLANES) or (cols // split) % LANES:
        split *= 2
        assert cols % split == 0, (arr.shape, n_blocks)
    return arr.reshape(rows * split, cols // split)


def _cast_specs(cast, n_blocks):
    views = [_row_blockable(arr, n_blocks) for arr in cast]
    specs = [pl.BlockSpec((v.shape[0] // n_blocks, v.shape[1]), lambda i: (i, 0)) for v in views]
    return views, specs


def _mod_kernel(ct_ref, w_ref, b_ref, *rest, n_cast):
    cast_src, o_ref, cast_dst = rest[:n_cast], rest[n_cast], rest[n_cast + 1:]

    n_batch, n_ada, width = o_ref.shape

    @pl.when(pl.program_id(0) == 0)
    def _():
        for j in range(n_ada):
            o_ref[:, j, :] = jnp.broadcast_to(b_ref[:, j * width:(j + 1) * width], (n_batch, width))

    for b in range(n_batch):
        col = ct_ref[:, b:b + 1]
        col = col * jax.nn.sigmoid(col)
        total = jnp.sum(w_ref[...] * col, axis=0, keepdims=True)
        for j in range(n_ada):
            o_ref[b, j:j + 1, :] += total[:, j * width:(j + 1) * width]
    for src, dst in zip(cast_src, cast_dst):
        dst[...] = src[...].astype(dst.dtype)


def _modulation(c, w_ada, b_ada, cast=()):
    n_batch = c.shape[0]
    n_cols = w_ada.shape[1]
    rows = D_MODEL // MOD_STEPS
    cast_shapes = [arr.shape for arr in cast]
    cast, cast_specs = _cast_specs(cast, MOD_STEPS)
    out = pl.pallas_call(
        functools.partial(_mod_kernel, n_cast=len(cast)),
        grid=(MOD_STEPS,),
        in_specs=[
            pl.BlockSpec((rows, n_batch), lambda j: (j, 0)),
            pl.BlockSpec((rows, n_cols), lambda j: (j, 0)),
            pl.BlockSpec((1, n_cols), lambda j: (0, 0)),
            *cast_specs,
        ],
        out_specs=[pl.BlockSpec((n_batch, N_ADA, D_MODEL), lambda j: (0, 0, 0)), *cast_specs],
        out_shape=[jax.ShapeDtypeStruct((n_batch, N_ADA, D_MODEL), jnp.float32),
                   *[jax.ShapeDtypeStruct(arr.shape, jnp.bfloat16) for arr in cast]],
        compiler_params=pltpu.CompilerParams(
            dimension_semantics=("arbitrary",), vmem_limit_bytes=VMEM_LIMIT_BYTES),
        name="adaln_modulation",
    )(c.T, w_ada, b_ada.reshape(1, n_cols), *cast)
    return out[0], [arr.reshape(shape) for arr, shape in zip(out[1:], cast_shapes)]


def _norm_modulate(x, g, shift, scale):
    ms = jnp.mean(x * x, axis=-1, keepdims=True)
    col_gain = g * (1.0 + scale)
    return (x * lax.rsqrt(ms + EPS) * col_gain + shift).astype(jnp.bfloat16)


def _ffn_kernel(x_ref, mod_ref, g_ref, win_ref, wout_ref, *rest, mod_row, n_cast):
    cast_src, o_ref, cast_dst = rest[:n_cast], rest[n_cast], rest[n_cast + 1:]
    x = x_ref[...]
    shift = mod_ref[0, mod_row:mod_row + 1, :]
    scale = mod_ref[0, mod_row + 1:mod_row + 2, :]
    gate = mod_ref[0, mod_row + 2:mod_row + 3, :]
    h = _norm_modulate(x, g_ref[...], shift, scale)
    acts = []
    for c in range(N_FF_CHUNKS):
        lo = c * FF_CHUNK
        a = jnp.dot(h, win_ref[:, lo:lo + FF_CHUNK], preferred_element_type=jnp.float32)
        b = jnp.dot(h, win_ref[:, D_FF + lo:D_FF + lo + FF_CHUNK], preferred_element_type=jnp.float32)
        acts.append((a * jax.nn.sigmoid(a) * b).astype(jnp.bfloat16))
    acc = jnp.dot(jnp.concatenate(acts, axis=1), wout_ref[...], preferred_element_type=jnp.float32)
    o_ref[...] = x + (0.5 * gate) * acc
    for src, dst in zip(cast_src, cast_dst):
        dst[...] = src[...].astype(dst.dtype)


def _ffn(x2d, mod, g, w_in, w_out, *, mod_row, tiles_per_batch, cast=()):
    n_rows = x2d.shape[0]
    n_steps = n_rows // FFN_ROW_TILE
    cast_shapes = [arr.shape for arr in cast]
    cast, cast_specs = _cast_specs(cast, n_steps)
    out = pl.pallas_call(
        functools.partial(_ffn_kernel, mod_row=mod_row, n_cast=len(cast)),
        grid=(n_steps,),
        in_specs=[
            pl.BlockSpec((FFN_ROW_TILE, D_MODEL), lambda i: (i, 0)),
            pl.BlockSpec((1, N_ADA, D_MODEL), lambda i: (i // tiles_per_batch, 0, 0)),
            _resident((1, D_MODEL)),
            _resident((D_MODEL, 2 * D_FF)),
            _resident((D_FF, D_MODEL)),
            *cast_specs,
        ],
        out_specs=[pl.BlockSpec((FFN_ROW_TILE, D_MODEL), lambda i: (i, 0)), *cast_specs],
        out_shape=[jax.ShapeDtypeStruct(x2d.shape, x2d.dtype),
                   *[jax.ShapeDtypeStruct(arr.shape, jnp.bfloat16) for arr in cast]],
        compiler_params=pltpu.CompilerParams(
            dimension_semantics=("arbitrary",), vmem_limit_bytes=VMEM_LIMIT_BYTES),
        name="swiglu_sublayer",
    )(x2d, mod, g.reshape(1, D_MODEL), w_in, w_out, *cast)
    return out[0], [arr.reshape(shape) for arr, shape in zip(out[1:], cast_shapes)]


def _head_rms_norm(t, head_gain):
    lane = lax.broadcasted_iota(jnp.int32, (t.shape[0], LANES), 1)
    low = lane < HEAD_DIM
    pair_gain = jnp.concatenate([head_gain, head_gain], axis=1)
    parts = []
    for p in range(HEAD_PAIRS):
        blk = t[:, p * LANES:(p + 1) * LANES]
        sq = blk * blk
        r_low = lax.rsqrt(jnp.sum(jnp.where(low, sq, 0.0), axis=-1, keepdims=True) * (1.0 / HEAD_DIM) + EPS)
        r_high = lax.rsqrt(jnp.sum(jnp.where(low, 0.0, sq), axis=-1, keepdims=True) * (1.0 / HEAD_DIM) + EPS)
        scaled = blk * jnp.where(low, r_low, r_high) * pair_gain
        parts.append(scaled.astype(jnp.bfloat16))
    return jnp.concatenate(parts, axis=-1)


def _build_bias_table(diag_ref, bias_buf):
    r = lax.broadcasted_iota(jnp.int32, (Q_BLOCK, BIAS_PERIOD), 0)
    j = lax.broadcasted_iota(jnp.int32, (Q_BLOCK, BIAS_PERIOD), 1)
    rel_chunk = r // CHUNK + LEFT_CHUNKS - j // CHUNK
    in_band = jnp.logical_and(rel_chunk >= 0, rel_chunk <= LEFT_CHUNKS)
    for head in range(N_HEADS):
        diag = jnp.broadcast_to(diag_ref[head:head + 1, :], (Q_BLOCK, BIAS_PERIOD))
        table = pltpu.roll(diag, 0, 1, stride=1, stride_axis=0)
        table = jnp.where(in_band, table * LOG2_E, MASK_VALUE)
        row0 = (head % 2) * Q_BLOCK
        bias_buf[head // 2, row0:row0 + Q_BLOCK, :] = table[:, :KEY_WINDOW]
    bias_buf[HEAD_PAIRS:2 * HEAD_PAIRS] = jnp.full((HEAD_PAIRS, 2 * Q_BLOCK, KEY_WINDOW), MASK_VALUE,
                                                   bias_buf.dtype)


def _mix_kernel(x_ref, mod_ref, g_ref, win_ref, qg_ref, kg_ref, diag_ref, wa_ref, wg_ref,
                ps_ref, wp_ref, wo_ref, o_ref, k_buf, v_buf, u_buf, a_buf, bias_buf, wpool_buf):
    tile = pl.program_id(1)
    first = tile == 0

    @pl.when(jnp.logical_and(pl.program_id(0) == 0, first))
    def _():
        _build_bias_table(diag_ref, bias_buf)
        for gi in range(POOL_GROUPS):
            rows = slice(gi * POOL_GROUP_DIM, (gi + 1) * POOL_GROUP_DIM)
            scaled = (wg_ref[rows, :].astype(jnp.float32) * ps_ref[:, rows]).astype(jnp.bfloat16)
            wpool_buf[rows, :] = jnp.dot(scaled, wp_ref[rows, :],
                                         preferred_element_type=jnp.float32).astype(wpool_buf.dtype)
        for p in range(HEAD_PAIRS):
            v_buf[ROW_TILE:2 * ROW_TILE, p * V_GROUP + LANES:(p + 1) * V_GROUP] = jnp.ones(
                (ROW_TILE, LANES), v_buf.dtype)

    @pl.when(first)
    def _():
        k_buf[0:ROW_TILE, :] = jnp.zeros((ROW_TILE, ATTN_WIDTH), k_buf.dtype)
        v_buf[0:ROW_TILE, :] = jnp.zeros((ROW_TILE, HEAD_PAIRS * V_GROUP), v_buf.dtype)
        u_buf[0:POOL_HISTORY, :] = jnp.zeros((POOL_HISTORY, POOL_WIDTH), u_buf.dtype)

    x = x_ref[...]
    shift = mod_ref[0, 3:4, :]
    scale = mod_ref[0, 4:5, :]
    gate = mod_ref[0, 5:6, :]
    h = _norm_modulate(x, g_ref[...], shift, scale)

    def proj(col0, width):
        return jnp.dot(h, win_ref[:, col0:col0 + width], preferred_element_type=jnp.float32)

    u = proj(3 * ATTN_WIDTH, POOL_WIDTH)
    u_buf[POOL_HISTORY:POOL_HISTORY + ROW_TILE, :] = u
    pos = tile * ROW_TILE + lax.broadcasted_iota(jnp.int32, (ROW_TILE, 1), 0)
    sums = u_buf[...]
    pooled = []
    span = 1
    for gi, window in enumerate(POOL_WINDOWS):
        while span < window:
            sums = sums + pltpu.roll(sums, span, axis=0)
            span *= 2
        total = sums[POOL_HISTORY:, :POOL_GROUP_DIM]
        if gi + 1 < POOL_GROUPS:
            sums = sums[:, POOL_GROUP_DIM:]
        count = jnp.minimum(pos + 1, window).astype(jnp.float32)
        u_group = u[:, gi * POOL_GROUP_DIM:(gi + 1) * POOL_GROUP_DIM]
        pooled.append((total / count - u_group).astype(jnp.bfloat16))
    pooled = jnp.concatenate(pooled, axis=-1)
    u_buf[0:POOL_HISTORY, :] = u_buf[ROW_TILE:ROW_TILE + POOL_HISTORY, :]

    q = proj(0, ATTN_WIDTH)
    qn = _head_rms_norm(q, qg_ref[...] * (HEAD_DIM ** -0.5 * LOG2_E))
    k = proj(ATTN_WIDTH, ATTN_WIDTH)
    k_buf[ROW_TILE:2 * ROW_TILE, :] = _head_rms_norm(k, kg_ref[...])
    v = proj(2 * ATTN_WIDTH, ATTN_WIDTH)
    for p in range(HEAD_PAIRS):
        v_buf[ROW_TILE:2 * ROW_TILE, p * V_GROUP:p * V_GROUP + LANES] = (
            v[:, p * LANES:(p + 1) * LANES].astype(jnp.bfloat16))

    lane = lax.broadcasted_iota(jnp.int32, (Q_BLOCK, LANES), 1)
    low = lane < HEAD_DIM
    masked_table = first.astype(jnp.int32) * HEAD_PAIRS

    def scores(half, p):
        base = half * QK_ROWS
        cols = slice(p * LANES, (p + 1) * LANES)
        stack = []
        for blk in range(QK_ROWS // Q_BLOCK):
            q_pair = qn[base + blk * Q_BLOCK:base + (blk + 1) * Q_BLOCK, cols]
            zero = jnp.zeros_like(q_pair)
            stack += [jnp.where(low, q_pair, zero), jnp.where(low, zero, q_pair)]
        return lax.dot_general(jnp.concatenate(stack, axis=0), k_buf[base:base + QK_SPAN, cols],
                               (((1,), (1,)), ((), ())),
                               preferred_element_type=jnp.float32)

    def attend(half, p, s_all):
        cols = slice(p * LANES, (p + 1) * LANES)
        for blk in range(QK_ROWS // Q_BLOCK):
            row0 = half * QK_ROWS + blk * Q_BLOCK
            bias = jnp.concatenate(
                [bias_buf[(masked_table if c0 < ROW_TILE - row0 else 0) + p, :, c0:c0 + LANES]
                 for c0 in range(0, KEY_WINDOW, LANES)], axis=1)
            s = s_all[2 * blk * Q_BLOCK:2 * (blk + 1) * Q_BLOCK,
                      blk * Q_BLOCK:blk * Q_BLOCK + KEY_WINDOW] + bias
            m = jnp.max(s, axis=-1, keepdims=True)
            pexp = jnp.exp2(s - m).astype(jnp.bfloat16)
            o = jnp.dot(pexp, v_buf[row0:row0 + KEY_WINDOW, p * V_GROUP:(p + 1) * V_GROUP],
                        preferred_element_type=jnp.float32)
            o = o[:, :LANES] / o[:, LANES:]
            a_buf[row0:row0 + Q_BLOCK, cols] = jnp.where(
                low, o[:Q_BLOCK], o[Q_BLOCK:]).astype(a_buf.dtype)

    units = [(half, p) for half in range(ROW_TILE // QK_ROWS) for p in range(HEAD_PAIRS)]
    gate_col0 = 3 * ATTN_WIDTH + POOL_WIDTH
    gate_width = 2 * D_MODEL // len(units)
    gate_parts = [proj(gate_col0, gate_width)]
    s_next = scores(*units[0])
    for n, unit in enumerate(units):
        s_cur = s_next
        if n + 1 < len(units):
            s_next = scores(*units[n + 1])
            gate_parts.append(proj(gate_col0 + (n + 1) * gate_width, gate_width))
        attend(*unit, s_cur)
    gates = jnp.concatenate(gate_parts, axis=1)
    ya = jnp.dot(a_buf[...], wa_ref[...], preferred_element_type=jnp.float32)
    k_buf[0:ROW_TILE, :] = k_buf[ROW_TILE:2 * ROW_TILE, :]
    v_buf[0:ROW_TILE, :] = v_buf[ROW_TILE:2 * ROW_TILE, :]

    yb = jnp.dot(pooled, wpool_buf[...], preferred_element_type=jnp.float32)

    merged = (jax.nn.sigmoid(gates[:, :D_MODEL]) * ya
              + jax.nn.sigmoid(gates[:, D_MODEL:]) * yb).astype(jnp.bfloat16)
    o_ref[...] = x + gate * jnp.dot(merged, wo_ref[...], preferred_element_type=jnp.float32)


def _bias_diagonals(rel_bias):
    n_far = BAND - REL_CLIP + 1
    n_near = KEY_WINDOW - n_far
    far = rel_bias[:, 2 * REL_CLIP:]
    near = rel_bias[:, 2 * REL_CLIP - n_near:2 * REL_CLIP][:, ::-1]
    return jnp.concatenate(
        [jnp.broadcast_to(far, (N_HEADS, n_far)), near,
         jnp.broadcast_to(far, (N_HEADS, BIAS_PERIOD - KEY_WINDOW))], axis=1).astype(jnp.float32)


def _token_mix(x3d, mod, g, w_in, q_gain, k_gain, rel_bias, w_attn_out, w_pool_group, pool_scale,
               w_pool_out, w_o):
    n_batch, seq, _ = x3d.shape
    bf16 = jnp.bfloat16
    in_cols = w_in.shape[1]
    return pl.pallas_call(
        _mix_kernel,
        grid=(n_batch, seq // ROW_TILE),
        in_specs=[
            pl.BlockSpec((None, ROW_TILE, D_MODEL), lambda b, i: (b, i, 0)),
            pl.BlockSpec((1, N_ADA, D_MODEL), lambda b, i: (b, 0, 0)),
            _resident((1, D_MODEL)),
            _resident((D_MODEL, in_cols)),
            _resident((1, HEAD_DIM)),
            _resident((1, HEAD_DIM)),
            _resident((N_HEADS, BIAS_PERIOD)),
            _resident((ATTN_WIDTH, D_MODEL)),
            _resident((POOL_WIDTH, POOL_GROUP_DIM)),
            _resident((1, POOL_WIDTH)),
            _resident((POOL_WIDTH, D_MODEL)),
            _resident((D_MODEL, D_MODEL)),
        ],
        out_specs=pl.BlockSpec((None, ROW_TILE, D_MODEL), lambda b, i: (b, i, 0)),
        out_shape=jax.ShapeDtypeStruct(x3d.shape, x3d.dtype),
        scratch_shapes=[
            pltpu.VMEM((2 * ROW_TILE, ATTN_WIDTH), bf16),
            pltpu.VMEM((2 * ROW_TILE, HEAD_PAIRS * V_GROUP), bf16),
            pltpu.VMEM((POOL_HISTORY + ROW_TILE, POOL_WIDTH), jnp.float32),
            pltpu.VMEM((ROW_TILE, ATTN_WIDTH), bf16),
            pltpu.VMEM((2 * HEAD_PAIRS, 2 * Q_BLOCK, KEY_WINDOW), jnp.float32),
            pltpu.VMEM((POOL_WIDTH, D_MODEL), bf16),
        ],
        compiler_params=pltpu.CompilerParams(
            dimension_semantics=("arbitrary", "arbitrary"), vmem_limit_bytes=VMEM_LIMIT_BYTES),
        name="token_mix_sublayer",
    )(x3d, mod, g.reshape(1, D_MODEL), w_in,
      q_gain.reshape(1, HEAD_DIM), k_gain.reshape(1, HEAD_DIM),
      _bias_diagonals(rel_bias), w_attn_out, w_pool_group, pool_scale.reshape(1, POOL_WIDTH), w_pool_out, w_o)


def kernel(x, c, w_ada, b_ada, g_ffn1, w_ffn1_in, w_ffn1_out, g_mix, w_in, q_gain, k_gain, rel_bias,
           w_attn_out, w_pool_group, pool_scale, w_pool_out, w_o, g_ffn2, w_ffn2_in, w_ffn2_out):
    n_batch, seq, d_model = x.shape
    depth = w_ada.shape[0]
    assert d_model == D_MODEL and seq % ROW_TILE == 0 and seq % FFN_ROW_TILE == 0
    tiles_per_batch = seq // FFN_ROW_TILE
    for l in range(depth):
        mod, (w_ffn1_in_b, w_ffn1_out_b) = _modulation(c, w_ada[l], b_ada[l],
                                                        cast=(w_ffn1_in[l], w_ffn1_out[l]))
        x2d = x.reshape(n_batch * seq, D_MODEL)
        later = (w_in[l], w_attn_out[l], w_pool_group[l].reshape(POOL_WIDTH, POOL_GROUP_DIM),
                 w_pool_out[l], w_o[l], w_ffn2_in[l], w_ffn2_out[l])
        x2d, later = _ffn(x2d, mod, g_ffn1[l], w_ffn1_in_b, w_ffn1_out_b, mod_row=0,
                          tiles_per_batch=tiles_per_batch, cast=later)
        w_in_b, w_attn_out_b, w_pool_group_b, w_pool_out_b, w_o_b, w_ffn2_in_b, w_ffn2_out_b = later
        x = _token_mix(x2d.reshape(n_batch, seq, D_MODEL), mod, g_mix[l], w_in_b, q_gain[l], k_gain[l],
                       rel_bias[l], w_attn_out_b, w_pool_group_b, pool_scale[l], w_pool_out_b, w_o_b)
        x2d, _ = _ffn(x.reshape(n_batch * seq, D_MODEL), mod, g_ffn2[l], w_ffn2_in_b, w_ffn2_out_b,
                      mod_row=6, tiles_per_batch=tiles_per_batch)
        x = x2d.reshape(n_batch, seq, D_MODEL)
    return x
```
